```python
import jax, jax.numpy as jnp
from jax import lax
import numpy as np

D_MODEL = 2048
BATCH = 4
SEQ = 8192
DEPTH = 1

RNN_WIDTH = D_MODEL
RNN_BLOCKS = 16
CONV_WIDTH = 4
RG_C = 8.0
N_HEADS = 16
HEAD_DIM = D_MODEL // N_HEADS
KV_RANK = 512
IDX_HEADS = 16
IDX_DIM = 64
TOPK_MAX = 256
Q_BLOCK = 128
PEER_HEADS = 8
PEER_KEYS = 128
PEER_EXPERTS = PEER_KEYS * PEER_KEYS
PEER_QDIM = 256
PEER_TOPK = 16
PEER_CHUNK = 128
EPS = 1e-6

SPLITS = (RNN_WIDTH, N_HEADS * HEAD_DIM, KV_RANK, IDX_HEADS * IDX_DIM, IDX_DIM, IDX_HEADS, D_MODEL, D_MODEL)
IN_WIDTH = sum(SPLITS)

kernel_name = "hybrid_rglru_dsa_peer"


def rmsnorm(x, g):
    xf = x.astype(jnp.float32)
    y = xf * lax.rsqrt(jnp.mean(xf * xf, axis=-1, keepdims=True) + EPS)
    return (y * g.astype(jnp.float32)).astype(x.dtype)


def layernorm(x, g, b):
    xf = x.astype(jnp.float32)
    mu = jnp.mean(xf, axis=-1, keepdims=True)
    var = jnp.mean(jnp.square(xf - mu), axis=-1, keepdims=True)
    y = (xf - mu) * lax.rsqrt(var + EPS)
    return (y * g.astype(jnp.float32) + b.astype(jnp.float32)).astype(x.dtype)


def causal_conv(x, w, b):
    s = x.shape[1]
    xp = jnp.pad(x, ((0, 0), (CONV_WIDTH - 1, 0), (0, 0)))
    y = b
    for i in range(CONV_WIDTH):
        y = y + xp[:, i:i + s] * w[i]
    return y


def rg_lru(x, wa, ba, wx, bx, lam):
    bsz, s, _ = x.shape
    xb = x.reshape(bsz, s, RNN_BLOCKS, -1)
    r = jax.nn.sigmoid(jnp.einsum('bsnc,ncd->bsnd', xb, wa).reshape(bsz, s, -1) + ba)
    i = jax.nn.sigmoid(jnp.einsum('bsnc,ncd->bsnd', xb, wx).reshape(bsz, s, -1) + bx)
    log_a = -RG_C * r.astype(jnp.float32) * jax.nn.softplus(-lam.astype(jnp.float32))
    a = jnp.exp(log_a)
    u = jnp.sqrt(-jnp.expm1(2.0 * log_a)) * (i * x).astype(jnp.float32)

    def combine(c1, c2):
        a1, b1 = c1
        a2, b2 = c2
        return a1 * a2, a2 * b1 + b2

    _, h = lax.associative_scan(combine, (a, u), axis=1)
    return h.astype(x.dtype)


def dsa_attention(q, c_kv, q_idx, k_idx, w_idx, w_uk, w_uv):
    bsz, s = c_kv.shape[:2]
    topk = min(TOPK_MAX, s // 4)
    key_pos = jnp.arange(s)

    def block(blk):
        t0 = blk * Q_BLOCK
        sl = lambda a: lax.dynamic_slice_in_dim(a, t0, Q_BLOCK, axis=1)
        qb, qib, wib = sl(q), sl(q_idx), sl(w_idx)
        qpos = t0 + jnp.arange(Q_BLOCK)
        rel = jax.nn.relu(jnp.einsum('bqhd,bsd->bqhs', qib, k_idx) * IDX_DIM ** -0.5)
        iscore = jnp.einsum('bqhs,bqh->bqs', rel, wib).astype(jnp.float32)
        causal = key_pos[None, :] <= qpos[:, None]
        iscore = jnp.where(causal[None], iscore, -jnp.inf)
        _, sel = lax.top_k(iscore, topk)
        valid = sel <= qpos[None, :, None]
        c_sel = jax.vmap(lambda c, ix: c[ix])(c_kv, sel)
        q_lat = jnp.einsum('bqhd,hcd->bqhc', qb, w_uk)
        logits = jnp.einsum('bqhc,bqkc->bqhk', q_lat, c_sel).astype(jnp.float32) * HEAD_DIM ** -0.5
        logits = jnp.where(valid[:, :, None, :], logits, -jnp.inf)
        p = jax.nn.softmax(logits, axis=-1).astype(c_sel.dtype)
        o_lat = jnp.einsum('bqhk,bqkc->bqhc', p, c_sel)
        o = jnp.einsum('bqhc,hcd->bqhd', o_lat, w_uv)
        return o.reshape(bsz, Q_BLOCK, -1)

    out = lax.map(block, jnp.arange(s // Q_BLOCK))
    return out.transpose(1, 0, 2, 3).reshape(bsz, s, -1)


def peer(x, wq, keys1, keys2, u, v):
    bsz, s, d = x.shape
    t = bsz * s
    xf = x.reshape(t, d)
    q = (xf @ wq).reshape(t, PEER_HEADS, PEER_QDIM)
    half = PEER_QDIM // 2
    s1 = jnp.einsum('thd,kd->thk', q[..., :half], keys1)
    s2 = jnp.einsum('thd,kd->thk', q[..., half:], keys2)
    v1, i1 = lax.top_k(s1, PEER_TOPK)
    v2, i2 = lax.top_k(s2, PEER_TOPK)
    cand_s = (v1[..., :, None] + v2[..., None, :]).reshape(t, PEER_HEADS, -1)
    cand_i = (i1[..., :, None] * PEER_KEYS + i2[..., None, :]).reshape(t, PEER_HEADS, -1)
    top_s, pos = lax.top_k(cand_s, PEER_TOPK)
    expert = jnp.take_along_axis(cand_i, pos, axis=-1).reshape(t, -1)
    gate = jax.nn.softmax(top_s.astype(jnp.float32), axis=-1).astype(x.dtype).reshape(t, -1)
    n_chunks = t // PEER_CHUNK

    def chunk(args):
        xc, ec, gc = args
        z = jnp.einsum('cd,ced->ce', xc, u[ec])
        act = jax.nn.gelu(z, approximate=False) * gc
        return jnp.einsum('ce,ced->cd', act, v[ec])

    out = lax.map(chunk, (xf.reshape(n_chunks, PEER_CHUNK, d),
                          expert.reshape(n_chunks, PEER_CHUNK, -1),
                          gate.reshape(n_chunks, PEER_CHUNK, -1)))
    return out.reshape(bsz, s, d)


def setup_inputs(seed: int = 0) -> dict:
    key = jax.random.key(seed)
    ks = jax.random.split(key, 24)
    f32 = jnp.float32
    nrm = lambda k, shape, scale: jax.random.normal(k, shape, f32) * scale
    gain = lambda k, shape: 1.0 + 0.05 * jax.random.normal(k, shape, f32)
    bw = RNN_WIDTH // RNN_BLOCKS
    a_c = jax.random.uniform(ks[9], (DEPTH, RNN_WIDTH), f32, 0.9, 0.999)
    a = a_c ** (1.0 / RG_C)
    rg_lambda = jnp.log(a) - jnp.log1p(-a)
    return {
        "x": nrm(ks[0], (BATCH, SEQ, D_MODEL), 1.0),
        "norm_mix_g": gain(ks[1], (DEPTH, D_MODEL)),
        "w_in": nrm(ks[2], (DEPTH, D_MODEL, IN_WIDTH), D_MODEL ** -0.5),
        "conv_w": nrm(ks[3], (DEPTH, CONV_WIDTH, RNN_WIDTH), CONV_WIDTH ** -0.5),
        "conv_b": nrm(ks[4], (DEPTH, RNN_WIDTH), 0.02),
        "rg_wa": nrm(ks[5], (DEPTH, RNN_BLOCKS, bw, bw), bw ** -0.5),
        "rg_ba": nrm(ks[6], (DEPTH, RNN_WIDTH), 0.1),
        "rg_wx": nrm(ks[7], (DEPTH, RNN_BLOCKS, bw, bw), bw ** -0.5),
        "rg_bx": nrm(ks[8], (DEPTH, RNN_WIDTH), 0.1),
        "rg_lambda": rg_lambda,
        "kv_norm_g": gain(ks[10], (DEPTH, KV_RANK)),
        "w_uk": nrm(ks[11], (DEPTH, N_HEADS, KV_RANK, HEAD_DIM), KV_RANK ** -0.5),
        "w_uv": nrm(ks[12], (DEPTH, N_HEADS, KV_RANK, HEAD_DIM), KV_RANK ** -0.5),
        "idx_ln_g": gain(ks[13], (DEPTH, IDX_DIM)),
        "idx_ln_b": nrm(ks[14], (DEPTH, IDX_DIM), 0.02),
        "w_o": nrm(ks[15], (DEPTH, D_MODEL, D_MODEL), D_MODEL ** -0.5),
        "norm_ffn_g": gain(ks[16], (DEPTH, D_MODEL)),
        "peer_wq": nrm(ks[17], (DEPTH, D_MODEL, PEER_HEADS * PEER_QDIM), D_MODEL ** -0.5),
        "peer_keys1": nrm(ks[18], (DEPTH, PEER_KEYS, PEER_QDIM // 2), (PEER_QDIM // 2) ** -0.5),
        "peer_keys2": nrm(ks[19], (DEPTH, PEER_KEYS, PEER_QDIM // 2), (PEER_QDIM // 2) ** -0.5),
        "peer_u": nrm(ks[20], (DEPTH, PEER_EXPERTS, D_MODEL), D_MODEL ** -0.5),
        "peer_v": nrm(ks[21], (DEPTH, PEER_EXPERTS, D_MODEL), PEER_HEADS ** -0.5),
        "norm_final_g": gain(ks[22], (D_MODEL,)),
    }


def reference(x, norm_mix_g, w_in, conv_w, conv_b, rg_wa, rg_ba, rg_wx, rg_bx, rg_lambda,
              kv_norm_g, w_uk, w_uv, idx_ln_g, idx_ln_b, w_o, norm_ffn_g, peer_wq,
              peer_keys1, peer_keys2, peer_u, peer_v, norm_final_g):
    bsz, s, _ = x.shape
    split_points = [int(p) for p in np.cumsum(SPLITS)[:-1]]
    for l in range(DEPTH):
        h = rmsnorm(x, norm_mix_g[l])
        proj = h @ w_in[l]
        xr, q, ckv, qi, ki, wi, gr, ga = jnp.split(proj, split_points, axis=-1)
        y_rnn = rg_lru(causal_conv(xr, conv_w[l], conv_b[l]),
                       rg_wa[l], rg_ba[l], rg_wx[l], rg_bx[l], rg_lambda[l])
        ckv = rmsnorm(ckv, kv_norm_g[l])
        ki = layernorm(ki, idx_ln_g[l], idx_ln_b[l])
        wi = wi * IDX_HEADS ** -0.5
        y_attn = dsa_attention(q.reshape(bsz, s, N_HEADS, HEAD_DIM), ckv,
                               qi.reshape(bsz, s, IDX_HEADS, IDX_DIM), ki, wi, w_uk[l], w_uv[l])
        mixed = jax.nn.sigmoid(gr) * y_rnn + jax.nn.sigmoid(ga) * y_attn
        x = x + mixed @ w_o[l]
        x = x + peer(rmsnorm(x, norm_ffn_g[l]), peer_wq[l], peer_keys1[l], peer_keys2[l],
                     peer_u[l], peer_v[l])
    return rmsnorm(x, norm_final_g)
```

```python
import functools

import jax
import jax.numpy as jnp
import numpy as np
from jax import lax
from jax.experimental import pallas as pl
from jax.experimental.pallas import tpu as pltpu

D_MODEL = 2048
RNN_BLOCKS = 16
RNN_BW = D_MODEL // RNN_BLOCKS
CONV_WIDTH = 4
RG_C = 8.0
N_HEADS = 16
HEAD_DIM = 128
KV_RANK = 512
IDX_HEADS = 16
IDX_DIM = 64
TOPK_MAX = 256
Q_BLOCK = 128
PEER_HEADS = 8
PEER_KEYS = 128
PEER_QDIM = 256
PEER_TOPK = 16
PEER_SLOTS = PEER_HEADS * PEER_TOPK
EPS = 1e-6

OFF_XR, OFF_Q, OFF_GR, OFF_GA = 0, 2048, 4096, 6144
OFF_QI, OFF_CKV, OFF_KI, OFF_WI = 8192, 9216, 9728, 9856
N_PROJ = 9984

V7X_VMEM_LIMIT = 56 * 1024 * 1024
INT_MIN = -(2 ** 31)
NEG_BIG = -1e30

BF16 = jnp.bfloat16
F32 = jnp.float32
NT_DIMS = (((1,), (1,)), ((), ()))


def _mm(a, b):
    return jnp.dot(a, b, preferred_element_type=F32)


def _mm_nt(a, b):
    return lax.dot_general(a, b, NT_DIMS, preferred_element_type=F32)


def _cparams(sem):
    return pltpu.CompilerParams(dimension_semantics=sem, vmem_limit_bytes=V7X_VMEM_LIMIT)


def _inproj_kernel(x_ref, g_ref, w_ref, o_ref, xn_ref):
    @pl.when(pl.program_id(1) == 0)
    def _():
        x = x_ref[...]
        ms = jnp.mean(x * x, axis=-1, keepdims=True)
        xn_ref[...] = ((x * lax.rsqrt(ms + EPS)) * g_ref[...]).astype(BF16)

    o_ref[...] = _mm(xn_ref[...], w_ref[...])


def _inproj(x2d, g, w_r, tm, tn):
    t, d = x2d.shape
    n = w_r.shape[1]
    return pl.pallas_call(
        _inproj_kernel,
        grid=(t // tm, n // tn),
        in_specs=[
            pl.BlockSpec((tm, d), lambda i, j: (i, 0)),
            pl.BlockSpec((1, d), lambda i, j: (0, 0)),
            pl.BlockSpec((d, tn), lambda i, j: (0, j)),
        ],
        out_specs=pl.BlockSpec((tm, tn), lambda i, j: (i, j)),
        out_shape=jax.ShapeDtypeStruct((t, n), F32),
        scratch_shapes=[pltpu.VMEM((tm, d), BF16)],
        compiler_params=_cparams(("parallel", "arbitrary")),
        name="inproj",
    )(x2d, g, w_r)


def _rglru_kernel(x_ref, cw_ref, cb_ref, wa_ref, ba_ref, wx_ref, bx_ref, lam_ref,
                  o_ref, xprev_ref, hprev_ref):
    @pl.when(pl.program_id(2) == 0)
    def _():
        xprev_ref[...] = jnp.zeros_like(xprev_ref)
        hprev_ref[...] = jnp.zeros_like(hprev_ref)

    x = x_ref[...]
    tc = x.shape[0]
    prev8 = xprev_ref[...]
    rows8 = lax.broadcasted_iota(jnp.int32, (8, RNN_BW), 0)
    y = cb_ref[...] + cw_ref[CONV_WIDTH - 1:CONV_WIDTH, :] * x
    for k in range(1, CONV_WIDTH):
        r = pltpu.roll(x, k, 0)
        pr = pltpu.roll(prev8, k, 0)
        top = jnp.where(rows8 < k, pr, r[:8])
        xs = jnp.concatenate([top, r[8:]], axis=0)
        y = y + cw_ref[CONV_WIDTH - 1 - k:CONV_WIDTH - k, :] * xs
    xprev_ref[...] = x[tc - 8:, :]

    xb = y.astype(BF16)
    r_g = jax.nn.sigmoid(_mm(xb, wa_ref[0]) + ba_ref[...])
    i_g = jax.nn.sigmoid(_mm(xb, wx_ref[0]) + bx_ref[...])
    nl = -lam_ref[...]
    sp = jnp.maximum(nl, 0.0) + jnp.log1p(jnp.exp(-jnp.abs(nl)))
    log_a = (-RG_C) * r_g * sp
    a = jnp.exp(log_a)
    u = jnp.sqrt(-jnp.tanh(log_a) * (a * a + 1.0)) * (i_g * y)

    rows = lax.broadcasted_iota(jnp.int32, (tc, RNN_BW), 0)
    d = 1
    while d < tc:
        a_sh = pltpu.roll(a, d, 0)
        u_sh = pltpu.roll(u, d, 0)
        m = rows >= d
        u = jnp.where(m, a * u_sh + u, u)
        a = jnp.where(m, a * a_sh, a)
        d *= 2
    h = u + a * hprev_ref[...]
    o_ref[...] = h
    hprev_ref[...] = h[tc - 1:tc, :]


def _rglru(proj, conv_w, conv_b, wa, ba, wx, bx, lam, bsz, s, tc):
    t = bsz * s
    nc = s // tc
    vec = lambda: pl.BlockSpec((1, RNN_BW), lambda b, n, c: (0, n))
    return pl.pallas_call(
        _rglru_kernel,
        grid=(bsz, RNN_BLOCKS, nc),
        in_specs=[
            pl.BlockSpec((tc, RNN_BW), lambda b, n, c: (b * nc + c, n)),
            pl.BlockSpec((CONV_WIDTH, RNN_BW), lambda b, n, c: (0, n)),
            vec(),
            pl.BlockSpec((1, RNN_BW, RNN_BW), lambda b, n, c: (n, 0, 0)),
            vec(),
            pl.BlockSpec((1, RNN_BW, RNN_BW), lambda b, n, c: (n, 0, 0)),
            vec(),
            vec(),
        ],
        out_specs=pl.BlockSpec((tc, RNN_BW), lambda b, n, c: (b * nc + c, n)),
        out_shape=jax.ShapeDtypeStruct((t, D_MODEL), F32),
        scratch_shapes=[pltpu.VMEM((8, RNN_BW), F32), pltpu.VMEM((1, RNN_BW), F32)],
        compiler_params=_cparams(("parallel", "parallel", "arbitrary")),
        name="rglru",
    )(proj, conv_w, conv_b, wa, ba, wx, bx, lam)


def _prep_kernel(ckv_ref, ki_ref, wi_ref, kvg_ref, lng_ref, lnb_ref,
                 ckvn_ref, ckvt_ref, kin_ref, wit_ref):
    c = ckv_ref[...]
    cn = (c * lax.rsqrt(jnp.mean(c * c, axis=-1, keepdims=True) + EPS)) * kvg_ref[...]
    ckvn_ref[...] = cn.astype(BF16)
    ckvt_ref[0] = cn.T.astype(BF16)
    k = ki_ref[...][:, :IDX_DIM]
    mu = jnp.mean(k, axis=-1, keepdims=True)
    var = jnp.mean(jnp.square(k - mu), axis=-1, keepdims=True)
    kn = (k - mu) * lax.rsqrt(var + EPS)
    kin_ref[...] = (kn * lng_ref[...] + lnb_ref[...]).astype(BF16)
    w = wi_ref[...] * (IDX_HEADS ** -0.5 * IDX_DIM ** -0.5)
    wit_ref[0] = w.T[:IDX_HEADS, :]


def _prep(proj, kvg, lng, lnb, bsz, s, tp):
    t = bsz * s
    nc = s // tp
    return pl.pallas_call(
        _prep_kernel,
        grid=(bsz, nc),
        in_specs=[
            pl.BlockSpec((tp, KV_RANK), lambda b, c: (b * nc + c, OFF_CKV // KV_RANK)),
            pl.BlockSpec((tp, 128), lambda b, c: (b * nc + c, OFF_KI // 128)),
            pl.BlockSpec((tp, 128), lambda b, c: (b * nc + c, OFF_WI // 128)),
            pl.BlockSpec((1, KV_RANK), lambda b, c: (0, 0)),
            pl.BlockSpec((1, IDX_DIM), lambda b, c: (0, 0)),
            pl.BlockSpec((1, IDX_DIM), lambda b, c: (0, 0)),
        ],
        out_specs=[
            pl.BlockSpec((tp, KV_RANK), lambda b, c: (b * nc + c, 0)),
            pl.BlockSpec((1, KV_RANK, tp), lambda b, c: (b, 0, c)),
            pl.BlockSpec((tp, IDX_DIM), lambda b, c: (b * nc + c, 0)),
            pl.BlockSpec((1, IDX_HEADS, tp), lambda b, c: (b, 0, c)),
        ],
        out_shape=[
            jax.ShapeDtypeStruct((t, KV_RANK), BF16),
            jax.ShapeDtypeStruct((bsz, KV_RANK, s), BF16),
            jax.ShapeDtypeStruct((t, IDX_DIM), BF16),
            jax.ShapeDtypeStruct((bsz, IDX_HEADS, s), F32),
        ],
        compiler_params=_cparams(("parallel", "parallel")),
        name="prep",
    )(proj, proj, proj, kvg, lng, lnb)


def _dsa_kernel(q_ref, qi_ref, wit_ref, k_ref, ckv_ref, ckvt_ref, wuk_ref, wuvt_ref, o_ref,
                keys_sc, qi_sc, qlat_sc, acc_sc, m_sc, l_sc, alpha_sc, p_sc, *, ks, topk):
    qb = pl.program_id(1)
    t0 = qb * Q_BLOCK
    nkb = (t0 + Q_BLOCK + ks - 1) // ks
    hd = HEAD_DIM

    for h in range(N_HEADS):
        qh = q_ref[:, h * hd:(h + 1) * hd].astype(BF16)
        qlat_sc[:, h * Q_BLOCK:(h + 1) * Q_BLOCK] = _mm_nt(wuk_ref[h], qh).astype(BF16)
    for h in range(IDX_HEADS):
        qi_sc[h] = qi_ref[:, h * IDX_DIM:(h + 1) * IDX_DIM].astype(BF16)

    lane_q = t0 + lax.broadcasted_iota(jnp.int32, (ks, Q_BLOCK), 1)
    row_i = lax.broadcasted_iota(jnp.int32, (ks, Q_BLOCK), 0)

    def score_body(kb, carry):
        s0 = pl.multiple_of(kb * ks, ks)
        kblk = k_ref[0, pl.ds(s0, ks), :]
        acc = jnp.zeros((ks, Q_BLOCK), F32)
        for h in range(IDX_HEADS):
            r = _mm_nt(kblk, qi_sc[h])
            acc = acc + jnp.maximum(r, 0.0) * wit_ref[0, h:h + 1, :]
        bits = pltpu.bitcast(acc, jnp.int32)
        key = bits ^ ((bits >> 31) & jnp.int32(0x7FFFFFFF))
        key = jnp.where(s0 + row_i <= lane_q, key, jnp.int32(INT_MIN))
        keys_sc[pl.ds(s0, ks), :] = key
        return carry

    lax.fori_loop(0, nkb, score_body, 0)

    def count_ge(cand):
        def body(kb, c):
            s0 = pl.multiple_of(kb * ks, ks)
            blk = keys_sc[pl.ds(s0, ks), :]
            return c + jnp.sum(jnp.where(blk >= cand, 1.0, 0.0), axis=0, keepdims=True)
        return lax.fori_loop(0, nkb, body, jnp.zeros((1, Q_BLOCK), F32))

    def bit_body(i, tu):
        bit = lax.shift_left(jnp.int32(1), 31 - i)
        cand_u = tu | bit
        cnt = count_ge(cand_u ^ jnp.int32(INT_MIN))
        return jnp.where(cnt >= float(topk), cand_u, tu)

    tu = lax.fori_loop(0, 32, bit_body, jnp.zeros((1, Q_BLOCK), jnp.int32))
    tsel = jnp.maximum(tu ^ jnp.int32(INT_MIN), jnp.int32(INT_MIN + 1))

    m_sc[...] = jnp.full_like(m_sc, NEG_BIG)
    l_sc[...] = jnp.zeros_like(l_sc)
    acc_sc[...] = jnp.zeros_like(acc_sc)
    scale = HEAD_DIM ** -0.5

    def attn_body(kb, carry):
        s0 = pl.multiple_of(kb * ks, ks)
        cblk = ckv_ref[0, pl.ds(s0, ks), :]
        logt = _mm(cblk, qlat_sc[...]) * scale
        sel = keys_sc[pl.ds(s0, ks), :] >= tsel
        for h in range(N_HEADS):
            sl = slice(h * Q_BLOCK, (h + 1) * Q_BLOCK)
            lg = logt[:, sl]
            mb = jnp.max(jnp.where(sel, lg, NEG_BIG), axis=0, keepdims=True)
            mo = m_sc[:, sl]
            mn = jnp.maximum(mo, mb)
            p = jnp.where(sel, jnp.exp(lg - mn), 0.0)
            alpha = jnp.exp(mo - mn)
            l_sc[:, sl] = alpha * l_sc[:, sl] + jnp.sum(p, axis=0, keepdims=True)
            m_sc[:, sl] = mn
            alpha_sc[:, sl] = alpha
            p_sc[:, sl] = p.astype(BF16)
        ctb = ckvt_ref[0, :, pl.ds(s0, ks)]
        acc_sc[...] = acc_sc[...] * alpha_sc[...] + _mm(ctb, p_sc[...])
        return carry

    lax.fori_loop(0, nkb, attn_body, 0)

    inv_l = 1.0 / l_sc[...]
    for h in range(N_HEADS):
        sl = slice(h * Q_BLOCK, (h + 1) * Q_BLOCK)
        ol = (acc_sc[:, sl] * inv_l[:, sl]).astype(BF16)
        ot = _mm(wuvt_ref[h], ol)
        o_ref[:, h * hd:(h + 1) * hd] = ot.T


def _dsa(proj, wit, kin, ckvn, ckvt, wuk, wuvt, bsz, s, ks):
    t = bsz * s
    nq = s // Q_BLOCK
    topk = min(TOPK_MAX, s // 4)
    one = pl.Buffered(1)
    kern = functools.partial(_dsa_kernel, ks=ks, topk=topk)
    return pl.pallas_call(
        kern,
        grid=(bsz, nq),
        in_specs=[
            pl.BlockSpec((Q_BLOCK, D_MODEL), lambda b, i: (b * nq + i, OFF_Q // D_MODEL)),
            pl.BlockSpec((Q_BLOCK, IDX_HEADS * IDX_DIM), lambda b, i: (b * nq + i, OFF_QI // 1024)),
            pl.BlockSpec((1, IDX_HEADS, Q_BLOCK), lambda b, i: (b, 0, i)),
            pl.BlockSpec((1, s, IDX_DIM), lambda b, i: (b, 0, 0), pipeline_mode=one),
            pl.BlockSpec((1, s, KV_RANK), lambda b, i: (b, 0, 0), pipeline_mode=one),
            pl.BlockSpec((1, KV_RANK, s), lambda b, i: (b, 0, 0), pipeline_mode=one),
            pl.BlockSpec((N_HEADS, KV_RANK, HEAD_DIM), lambda b, i: (0, 0, 0), pipeline_mode=one),
            pl.BlockSpec((N_HEADS, HEAD_DIM, KV_RANK), lambda b, i: (0, 0, 0), pipeline_mode=one),
        ],
        out_specs=pl.BlockSpec((Q_BLOCK, D_MODEL), lambda b, i: (b * nq + i, 0)),
        out_shape=jax.ShapeDtypeStruct((t, D_MODEL), F32),
        scratch_shapes=[
            pltpu.VMEM((s, Q_BLOCK), jnp.int32),
            pltpu.VMEM((IDX_HEADS, Q_BLOCK, IDX_DIM), BF16),
            pltpu.VMEM((KV_RANK, N_HEADS * Q_BLOCK), BF16),
            pltpu.VMEM((KV_RANK, N_HEADS * Q_BLOCK), F32),
            pltpu.VMEM((1, N_HEADS * Q_BLOCK), F32),
            pltpu.VMEM((1, N_HEADS * Q_BLOCK), F32),
            pltpu.VMEM((1, N_HEADS * Q_BLOCK), F32),
            pltpu.VMEM((ks, N_HEADS * Q_BLOCK), BF16),
        ],
        compiler_params=_cparams(("parallel", "arbitrary")),
        name="dsa",
    )(proj, proj, wit, kin.reshape(bsz, s, IDX_DIM), ckvn.reshape(bsz, s, KV_RANK), ckvt, wuk, wuvt)


def _mixout_kernel(gr_ref, ga_ref, yr_ref, ya_ref, x_ref, wo_ref, x1_ref):
    mixed = jax.nn.sigmoid(gr_ref[...]) * yr_ref[...] + jax.nn.sigmoid(ga_ref[...]) * ya_ref[...]
    x1_ref[...] = x_ref[...] + _mm(mixed.astype(BF16), wo_ref[...])


def _mixout(proj, y_rnn, y_attn, x2d, wo, tm):
    t, d = x2d.shape
    row = lambda i: (i, 0)
    return pl.pallas_call(
        _mixout_kernel,
        grid=(t // tm,),
        in_specs=[
            pl.BlockSpec((tm, d), lambda i: (i, OFF_GR // D_MODEL)),
            pl.BlockSpec((tm, d), lambda i: (i, OFF_GA // D_MODEL)),
            pl.BlockSpec((tm, d), row),
            pl.BlockSpec((tm, d), row),
            pl.BlockSpec((tm, d), row),
            pl.BlockSpec((d, d), lambda i: (0, 0), pipeline_mode=pl.Buffered(1)),
        ],
        out_specs=pl.BlockSpec((tm, d), row),
        out_shape=jax.ShapeDtypeStruct((t, d), F32),
        compiler_params=_cparams(("parallel",)),
        name="mixout",
    )(proj, proj, y_rnn, y_attn, x2d, wo)


def _peerq_kernel(x1_ref, g_ref, wq_ref, k1_ref, k2_ref, xn_ref, s1_ref, s2_ref):
    x = x1_ref[...]
    xn = (x * lax.rsqrt(jnp.mean(x * x, axis=-1, keepdims=True) + EPS)) * g_ref[...]
    xn_ref[...] = xn
    qp = _mm(xn.astype(BF16), wq_ref[...])
    half = PEER_QDIM // 2
    for h in range(PEER_HEADS):
        qa = qp[:, h * PEER_QDIM:h * PEER_QDIM + half].astype(BF16)
        qb = qp[:, h * PEER_QDIM + half:(h + 1) * PEER_QDIM].astype(BF16)
        s1_ref[h] = _mm_nt(k1_ref[...], qa)
        s2_ref[h] = _mm_nt(k2_ref[...], qb)


def _peerq(x1, g, wq, k1, k2, tm):
    t, d = x1.shape
    sc_spec = pl.BlockSpec((PEER_HEADS, PEER_KEYS, tm), lambda i: (0, 0, i))
    sc_shape = jax.ShapeDtypeStruct((PEER_HEADS, PEER_KEYS, t), F32)
    return pl.pallas_call(
        _peerq_kernel,
        grid=(t // tm,),
        in_specs=[
            pl.BlockSpec((tm, d), lambda i: (i, 0)),
            pl.BlockSpec((1, d), lambda i: (0, 0)),
            pl.BlockSpec((d, PEER_HEADS * PEER_QDIM), lambda i: (0, 0), pipeline_mode=pl.Buffered(1)),
            pl.BlockSpec((PEER_KEYS, PEER_QDIM // 2), lambda i: (0, 0)),
            pl.BlockSpec((PEER_KEYS, PEER_QDIM // 2), lambda i: (0, 0)),
        ],
        out_specs=[pl.BlockSpec((tm, d), lambda i: (i, 0)), sc_spec, sc_spec],
        out_shape=[jax.ShapeDtypeStruct((t, d), F32), sc_shape, sc_shape],
        compiler_params=_cparams(("parallel",)),
        name="peerq",
    )(x1, g, wq, k1, k2)


def _top16_rows(v, with_payload=None):
    n = v.shape[0]
    iota = lax.broadcasted_iota(jnp.int32, v.shape, 0)
    vals, idxs, pays = [], [], []
    for _ in range(PEER_TOPK):
        m = jnp.max(v, axis=0, keepdims=True)
        i = jnp.min(jnp.where(v == m, iota, n), axis=0, keepdims=True)
        hit = iota == i
        vals.append(m)
        idxs.append(i)
        if with_payload is not None:
            pays.append(jnp.sum(jnp.where(hit, with_payload, 0), axis=0, keepdims=True))
        v = jnp.where(hit, -jnp.inf, v)
    return vals, idxs, pays


def _route_kernel(s1_ref, s2_ref, ids_ref, g_ref):
    for h in range(PEER_HEADS):
        v1, i1, _ = _top16_rows(s1_ref[h])
        v2, i2, _ = _top16_rows(s2_ref[h])
        v2c = jnp.concatenate(v2, axis=0)
        i2c = jnp.concatenate(i2, axis=0)
        cand_s = jnp.concatenate([v1[a] + v2c for a in range(PEER_TOPK)], axis=0)
        cand_i = jnp.concatenate([i1[a] * PEER_KEYS + i2c for a in range(PEER_TOPK)], axis=0)
        top_s, _, experts = _top16_rows(cand_s, with_payload=cand_i)
        ts = jnp.concatenate(top_s, axis=0)
        e = jnp.exp(ts - ts[0:1, :])
        gate = e / jnp.sum(e, axis=0, keepdims=True)
        ids_ref[0, h * PEER_TOPK:(h + 1) * PEER_TOPK, :] = jnp.concatenate(experts, axis=0)
        g_ref[0, h * PEER_TOPK:(h + 1) * PEER_TOPK, :] = gate


def _route(s1t, s2t, tg):
    t = s1t.shape[2]
    ng = t // tg
    sc_spec = pl.BlockSpec((PEER_HEADS, PEER_KEYS, tg), lambda i: (0, 0, i))
    out_spec = pl.BlockSpec((1, PEER_SLOTS, tg), lambda i: (i, 0, 0))
    return pl.pallas_call(
        _route_kernel,
        grid=(ng,),
        in_specs=[sc_spec, sc_spec],
        out_specs=[out_spec, out_spec],
        out_shape=[jax.ShapeDtypeStruct((ng, PEER_SLOTS, tg), jnp.int32),
                   jax.ShapeDtypeStruct((ng, PEER_SLOTS, tg), F32)],
        compiler_params=_cparams(("parallel",)),
        name="route",
    )(s1t, s2t)


N_GATHER_BUFS = 3


def _experts_kernel(ids_ref, g_ref, xn_ref, x1_ref, gf_ref, uv_ref, y_ref, buf, sem, acc_ref, *, tg):
    lane_t = lax.broadcasted_iota(jnp.int32, (PEER_SLOTS, tg), 1)

    def row_copy(tok, j, slot):
        e = ids_ref[0, j, tok]
        return pltpu.make_async_copy(uv_ref.at[pl.ds(e, 1), :], buf.at[slot, pl.ds(j, 1), :], sem.at[slot])

    def start_token(tok, slot):
        for j in range(PEER_SLOTS):
            row_copy(tok, j, slot).start()

    def wait_token(slot):
        pltpu.make_async_copy(uv_ref.at[pl.ds(0, PEER_SLOTS), :], buf.at[slot], sem.at[slot]).wait()

    for p in range(N_GATHER_BUFS - 1):
        start_token(p, p)

    def token_body(tok, carry):
        slot = lax.rem(tok, N_GATHER_BUFS)
        nxt = tok + (N_GATHER_BUFS - 1)

        @pl.when(nxt < tg)
        def _():
            start_token(nxt, lax.rem(nxt, N_GATHER_BUFS))

        wait_token(slot)
        w = buf[slot]
        uf = pltpu.bitcast(w & jnp.uint32(0xFFFF0000), F32)
        vf = pltpu.bitcast(w << 16, F32)
        xrow = xn_ref[pl.ds(tok, 1), :]
        z = jnp.sum(uf * xrow, axis=-1, keepdims=True)
        gcol = jnp.sum(jnp.where(lane_t == tok, g_ref[0], 0.0), axis=-1, keepdims=True)
        act = (0.5 * z * (1.0 + lax.erf(z * (2.0 ** -0.5)))) * gcol
        acc_ref[pl.ds(tok, 1), :] = jnp.sum(act * vf, axis=0, keepdims=True)
        return carry

    lax.fori_loop(0, tg, token_body, 0)

    xf = x1_ref[...] + acc_ref[...]
    y_ref[...] = (xf * lax.rsqrt(jnp.mean(xf * xf, axis=-1, keepdims=True) + EPS)) * gf_ref[...]


def _experts(ids, gates, xn, x1, gf, uv, tg):
    t, d = x1.shape
    ng = t // tg
    kern = functools.partial(_experts_kernel, tg=tg)
    return pl.pallas_call(
        kern,
        grid=(ng,),
        in_specs=[
            pl.BlockSpec((1, PEER_SLOTS, tg), lambda i: (i, 0, 0), memory_space=pltpu.SMEM),
            pl.BlockSpec((1, PEER_SLOTS, tg), lambda i: (i, 0, 0)),
            pl.BlockSpec((tg, d), lambda i: (i, 0)),
            pl.BlockSpec((tg, d), lambda i: (i, 0)),
            pl.BlockSpec((1, d), lambda i: (0, 0)),
            pl.BlockSpec(memory_space=pl.ANY),
        ],
        out_specs=pl.BlockSpec((tg, d), lambda i: (i, 0)),
        out_shape=jax.ShapeDtypeStruct((t, d), F32),
        scratch_shapes=[
            pltpu.VMEM((N_GATHER_BUFS, PEER_SLOTS, d), jnp.uint32),
            pltpu.SemaphoreType.DMA((N_GATHER_BUFS,)),
            pltpu.VMEM((tg, d), F32),
        ],
        compiler_params=_cparams(("arbitrary",)),
        name="experts",
    )(ids, gates, xn, x1, gf, uv)


def _regroup_w_in(w):
    d = w.shape[0]
    z = lambda n: jnp.zeros((d, n), w.dtype)
    xr_q = w[:, 0:4096]
    ckv = w[:, 4096:4608]
    qi = w[:, 4608:5632]
    ki = w[:, 5632:5696]
    wi = w[:, 5696:5712]
    gr_ga = w[:, 5712:9808]
    return jnp.concatenate([xr_q, gr_ga, qi, ckv, ki, z(64), wi, z(112)], axis=1).astype(BF16)


def _pack_uv(u, v):
    ub = lax.bitcast_convert_type(u.astype(BF16), jnp.uint16).astype(jnp.uint32)
    vb = lax.bitcast_convert_type(v.astype(BF16), jnp.uint16).astype(jnp.uint32)
    return (ub << 16) | vb


def _layer(x, norm_mix_g, w_in, conv_w, conv_b, rg_wa, rg_ba, rg_wx, rg_bx, rg_lambda,
           kv_norm_g, w_uk, w_uv, idx_ln_g, idx_ln_b, w_o, norm_ffn_g, peer_wq,
           peer_keys1, peer_keys2, peer_u, peer_v, norm_final_g, *,
           tm_in, tn_in, tc_rnn, tp, ks, tm_mix, tm_pq, tg):
    bsz, s, d = x.shape
    t = bsz * s
    x2d = x.reshape(t, d)
    row = lambda a: a.reshape(1, -1)

    proj = _inproj(x2d, row(norm_mix_g), _regroup_w_in(w_in), tm_in, tn_in)
    y_rnn = _rglru(proj, conv_w, row(conv_b), rg_wa.astype(BF16), row(rg_ba),
                   rg_wx.astype(BF16), row(rg_bx), row(rg_lambda), bsz, s, tc_rnn)
    ckvn, ckvt, kin, wit = _prep(proj, row(kv_norm_g), row(idx_ln_g), row(idx_ln_b), bsz, s, tp)
    y_attn = _dsa(proj, wit, kin, ckvn, ckvt, w_uk.astype(BF16),
                  jnp.transpose(w_uv, (0, 2, 1)).astype(BF16), bsz, s, ks)
    x1 = _mixout(proj, y_rnn, y_attn, x2d, w_o.astype(BF16), tm_mix)
    xn, s1t, s2t = _peerq(x1, row(norm_ffn_g), peer_wq.astype(BF16),
                          peer_keys1.astype(BF16), peer_keys2.astype(BF16), tm_pq)
    ids, gates = _route(s1t, s2t, tg)
    y = _experts(ids, gates, xn, x1, row(norm_final_g), _pack_uv(peer_u, peer_v), tg)
    return y.reshape(bsz, s, d)


def kernel(x, norm_mix_g, w_in, conv_w, conv_b, rg_wa, rg_ba, rg_wx, rg_bx, rg_lambda,
           kv_norm_g, w_uk, w_uv, idx_ln_g, idx_ln_b, w_o, norm_ffn_g, peer_wq,
           peer_keys1, peer_keys2, peer_u, peer_v, norm_final_g):
    assert norm_mix_g.shape[0] == 1, "single-layer trunk"
    s = x.shape[1]
    return _layer(
        x, norm_mix_g[0], w_in[0], conv_w[0], conv_b[0], rg_wa[0], rg_ba[0], rg_wx[0], rg_bx[0],
        rg_lambda[0], kv_norm_g[0], w_uk[0], w_uv[0], idx_ln_g[0], idx_ln_b[0], w_o[0],
        norm_ffn_g[0], peer_wq[0], peer_keys1[0], peer_keys2[0], peer_u[0], peer_v[0], norm_final_g,
        tm_in=min(512, s), tn_in=1664, tc_rnn=min(512, s), tp=min(512, s), ks=min(512, s),
        tm_mix=min(256, s), tm_pq=min(256, s), tg=128)
```

```python
import functools

import jax
import jax.numpy as jnp
import numpy as np
from jax import lax
from jax.experimental import pallas as pl
from jax.experimental.pallas import tpu as pltpu

D_MODEL = 2048
RNN_BLOCKS = 16
RNN_BW = D_MODEL // RNN_BLOCKS
CONV_WIDTH = 4
RG_C = 8.0
N_HEADS = 16
HEAD_DIM = 128
KV_RANK = 512
IDX_HEADS = 16
IDX_DIM = 64
TOPK_MAX = 256
Q_BLOCK = 128
PEER_HEADS = 8
PEER_KEYS = 128
PEER_QDIM = 256
PEER_TOPK = 16
PEER_SLOTS = PEER_HEADS * PEER_TOPK
EPS = 1e-6

OFF_XR, OFF_Q, OFF_GR, OFF_GA = 0, 2048, 4096, 6144
OFF_QI, OFF_CKV, OFF_KI, OFF_WI = 8192, 9216, 9728, 9856
N_PROJ = 9984

V7X_VMEM_LIMIT = 56 * 1024 * 1024
COUNT_ROWS = 64
INT_MIN = -(2 ** 31)
NEG_BIG = -1e30

BF16 = jnp.bfloat16
F32 = jnp.float32
NT_DIMS = (((1,), (1,)), ((), ()))


def _mm(a, b):
    return jnp.dot(a, b, preferred_element_type=F32)


def _mm_nt(a, b):
    return lax.dot_general(a, b, NT_DIMS, preferred_element_type=F32)


def _cparams(sem):
    return pltpu.CompilerParams(dimension_semantics=sem, vmem_limit_bytes=V7X_VMEM_LIMIT)


def _inproj_kernel(x_ref, g_ref, w_ref, o_ref, xn_ref):
    @pl.when(pl.program_id(1) == 0)
    def _():
        x = x_ref[...]
        ms = jnp.mean(x * x, axis=-1, keepdims=True)
        xn_ref[...] = ((x * lax.rsqrt(ms + EPS)) * g_ref[...]).astype(BF16)

    o_ref[...] = _mm(xn_ref[...], w_ref[...])


def _inproj(x2d, g, w_r, tm, tn):
    t, d = x2d.shape
    n = w_r.shape[1]
    return pl.pallas_call(
        _inproj_kernel,
        grid=(t // tm, n // tn),
        in_specs=[
            pl.BlockSpec((tm, d), lambda i, j: (i, 0)),
            pl.BlockSpec((1, d), lambda i, j: (0, 0)),
            pl.BlockSpec((d, tn), lambda i, j: (0, j)),
        ],
        out_specs=pl.BlockSpec((tm, tn), lambda i, j: (i, j)),
        out_shape=jax.ShapeDtypeStruct((t, n), F32),
        scratch_shapes=[pltpu.VMEM((tm, d), BF16)],
        compiler_params=_cparams(("parallel", "arbitrary")),
        name="inproj",
    )(x2d, g, w_r)


def _rglru_kernel(x_ref, cw_ref, cb_ref, wa_ref, ba_ref, wx_ref, bx_ref, lam_ref,
                  o_ref, xprev_ref, hprev_ref):
    @pl.when(pl.program_id(2) == 0)
    def _():
        xprev_ref[...] = jnp.zeros_like(xprev_ref)
        hprev_ref[...] = jnp.zeros_like(hprev_ref)

    x = x_ref[...]
    tc = x.shape[0]
    prev8 = xprev_ref[...]
    rows8 = lax.broadcasted_iota(jnp.int32, (8, RNN_BW), 0)
    y = cb_ref[...] + cw_ref[CONV_WIDTH - 1:CONV_WIDTH, :] * x
    for k in range(1, CONV_WIDTH):
        r = pltpu.roll(x, k, 0)
        pr = pltpu.roll(prev8, k, 0)
        top = jnp.where(rows8 < k, pr, r[:8])
        xs = jnp.concatenate([top, r[8:]], axis=0)
        y = y + cw_ref[CONV_WIDTH - 1 - k:CONV_WIDTH - k, :] * xs
    xprev_ref[...] = x[tc - 8:, :]

    xb = y.astype(BF16)
    r_g = jax.nn.sigmoid(_mm(xb, wa_ref[0]) + ba_ref[...])
    i_g = jax.nn.sigmoid(_mm(xb, wx_ref[0]) + bx_ref[...])
    nl = -lam_ref[...]
    sp = jnp.maximum(nl, 0.0) + jnp.log1p(jnp.exp(-jnp.abs(nl)))
    log_a = (-RG_C) * r_g * sp
    a = jnp.exp(log_a)
    u = jnp.sqrt(-jnp.tanh(log_a) * (a * a + 1.0)) * (i_g * y)

    rows = lax.broadcasted_iota(jnp.int32, (tc, RNN_BW), 0)
    d = 1
    while d < tc:
        a_sh = pltpu.roll(a, d, 0)
        u_sh = pltpu.roll(u, d, 0)
        m = rows >= d
        u = jnp.where(m, a * u_sh + u, u)
        a = jnp.where(m, a * a_sh, a)
        d *= 2
    h = u + a * hprev_ref[...]
    o_ref[...] = h
    hprev_ref[...] = h[tc - 1:tc, :]


def _rglru(proj, conv_w, conv_b, wa, ba, wx, bx, lam, bsz, s, tc):
    t = bsz * s
    nc = s // tc
    vec = lambda: pl.BlockSpec((1, RNN_BW), lambda b, n, c: (0, n))
    return pl.pallas_call(
        _rglru_kernel,
        grid=(bsz, RNN_BLOCKS, nc),
        in_specs=[
            pl.BlockSpec((tc, RNN_BW), lambda b, n, c: (b * nc + c, n)),
            pl.BlockSpec((CONV_WIDTH, RNN_BW), lambda b, n, c: (0, n)),
            vec(),
            pl.BlockSpec((1, RNN_BW, RNN_BW), lambda b, n, c: (n, 0, 0)),
            vec(),
            pl.BlockSpec((1, RNN_BW, RNN_BW), lambda b, n, c: (n, 0, 0)),
            vec(),
            vec(),
        ],
        out_specs=pl.BlockSpec((tc, RNN_BW), lambda b, n, c: (b * nc + c, n)),
        out_shape=jax.ShapeDtypeStruct((t, D_MODEL), F32),
        scratch_shapes=[pltpu.VMEM((8, RNN_BW), F32), pltpu.VMEM((1, RNN_BW), F32)],
        compiler_params=_cparams(("parallel", "parallel", "arbitrary")),
        name="rglru",
    )(proj, conv_w, conv_b, wa, ba, wx, bx, lam)


def _prep_kernel(ckv_ref, ki_ref, wi_ref, kvg_ref, lng_ref, lnb_ref,
                 ckvn_ref, ckvt_ref, kin_ref, wit_ref):
    c = ckv_ref[...]
    cn = (c * lax.rsqrt(jnp.mean(c * c, axis=-1, keepdims=True) + EPS)) * kvg_ref[...]
    ckvn_ref[...] = cn.astype(BF16)
    ckvt_ref[0] = cn.T.astype(BF16)
    k = ki_ref[...][:, :IDX_DIM]
    mu = jnp.mean(k, axis=-1, keepdims=True)
    var = jnp.mean(jnp.square(k - mu), axis=-1, keepdims=True)
    kn = (k - mu) * lax.rsqrt(var + EPS)
    kin_ref[...] = (kn * lng_ref[...] + lnb_ref[...]).astype(BF16)
    w = wi_ref[...] * (IDX_HEADS ** -0.5 * IDX_DIM ** -0.5)
    wit_ref[0] = w.T[:IDX_HEADS, :]


def _prep(proj, kvg, lng, lnb, bsz, s, tp):
    t = bsz * s
    nc = s // tp
    return pl.pallas_call(
        _prep_kernel,
        grid=(bsz, nc),
        in_specs=[
            pl.BlockSpec((tp, KV_RANK), lambda b, c: (b * nc + c, OFF_CKV // KV_RANK)),
            pl.BlockSpec((tp, 128), lambda b, c: (b * nc + c, OFF_KI // 128)),
            pl.BlockSpec((tp, 128), lambda b, c: (b * nc + c, OFF_WI // 128)),
            pl.BlockSpec((1, KV_RANK), lambda b, c: (0, 0)),
            pl.BlockSpec((1, IDX_DIM), lambda b, c: (0, 0)),
            pl.BlockSpec((1, IDX_DIM), lambda b, c: (0, 0)),
        ],
        out_specs=[
            pl.BlockSpec((tp, KV_RANK), lambda b, c: (b * nc + c, 0)),
            pl.BlockSpec((1, KV_RANK, tp), lambda b, c: (b, 0, c)),
            pl.BlockSpec((tp, IDX_DIM), lambda b, c: (b * nc + c, 0)),
            pl.BlockSpec((1, IDX_HEADS, tp), lambda b, c: (b, 0, c)),
        ],
        out_shape=[
            jax.ShapeDtypeStruct((t, KV_RANK), BF16),
            jax.ShapeDtypeStruct((bsz, KV_RANK, s), BF16),
            jax.ShapeDtypeStruct((t, IDX_DIM), BF16),
            jax.ShapeDtypeStruct((bsz, IDX_HEADS, s), F32),
        ],
        compiler_params=_cparams(("parallel", "parallel")),
        name="prep",
    )(proj, proj, proj, kvg, lng, lnb)


def _dsa_kernel(q_ref, qi_ref, wit_ref, k_ref, ckv_ref, ckvt_ref, wuk_ref, wuvt_ref, o_ref,
                keys_sc, qi_sc, qlat_sc, acc_sc, m_sc, l_sc, alpha_sc, p_sc, *, ks, topk):
    qb = pl.program_id(1)
    t0 = qb * Q_BLOCK
    nkb = (t0 + Q_BLOCK + ks - 1) // ks
    hd = HEAD_DIM

    for h in range(N_HEADS):
        qh = q_ref[:, h * hd:(h + 1) * hd].astype(BF16)
        qlat_sc[:, h * Q_BLOCK:(h + 1) * Q_BLOCK] = _mm_nt(wuk_ref[h], qh).astype(BF16)
    for h in range(IDX_HEADS):
        qi_sc[h] = qi_ref[:, h * IDX_DIM:(h + 1) * IDX_DIM].astype(BF16)

    lane_q = t0 + lax.broadcasted_iota(jnp.int32, (ks, Q_BLOCK), 1)
    row_i = lax.broadcasted_iota(jnp.int32, (ks, Q_BLOCK), 0)

    def score_body(kb, carry):
        s0 = pl.multiple_of(kb * ks, ks)
        kblk = k_ref[0, pl.ds(s0, ks), :]
        acc = jnp.zeros((ks, Q_BLOCK), F32)
        for h in range(IDX_HEADS):
            r = _mm_nt(kblk, qi_sc[h])
            acc = acc + jnp.maximum(r, 0.0) * wit_ref[0, h:h + 1, :]
        bits = pltpu.bitcast(acc, jnp.int32)
        key = bits ^ ((bits >> 31) & jnp.int32(0x7FFFFFFF))
        key = jnp.where(s0 + row_i <= lane_q, key, jnp.int32(INT_MIN))
        keys_sc[pl.ds(s0, ks), :] = key
        return carry

    lax.fori_loop(0, nkb, score_body, 0)

    def count_ge(cand):
        def body(kb, c):
            s0 = pl.multiple_of(kb * ks, ks)
            blk = keys_sc[pl.ds(s0, ks), :]
            hit = jnp.where(blk >= cand, 1.0, 0.0)
            return c + jnp.sum(hit.reshape(ks // COUNT_ROWS, COUNT_ROWS, Q_BLOCK), axis=0)
        cpart = lax.fori_loop(0, nkb, body, jnp.zeros((COUNT_ROWS, Q_BLOCK), F32))
        return jnp.sum(cpart, axis=0, keepdims=True)

    def bit_body(i, tu):
        bit = lax.shift_left(jnp.int32(1), 31 - i)
        cand_u = tu | bit
        cnt = count_ge(cand_u ^ jnp.int32(INT_MIN))
        return jnp.where(cnt >= float(topk), cand_u, tu)

    tu = lax.fori_loop(0, 32, bit_body, jnp.zeros((1, Q_BLOCK), jnp.int32))
    tsel = jnp.maximum(tu ^ jnp.int32(INT_MIN), jnp.int32(INT_MIN + 1))

    m_sc[...] = jnp.full_like(m_sc, NEG_BIG)
    l_sc[...] = jnp.zeros_like(l_sc)
    acc_sc[...] = jnp.zeros_like(acc_sc)
    scale = HEAD_DIM ** -0.5

    def attn_body(kb, carry):
        s0 = pl.multiple_of(kb * ks, ks)
        cblk = ckv_ref[0, pl.ds(s0, ks), :]
        logt = _mm(cblk, qlat_sc[...]) * scale
        sel = keys_sc[pl.ds(s0, ks), :] >= tsel
        for h in range(N_HEADS):
            sl = slice(h * Q_BLOCK, (h + 1) * Q_BLOCK)
            lg = logt[:, sl]
            mb = jnp.max(jnp.where(sel, lg, NEG_BIG), axis=0, keepdims=True)
            mo = m_sc[:, sl]
            mn = jnp.maximum(mo, mb)
            p = jnp.where(sel, jnp.exp(lg - mn), 0.0)
            alpha = jnp.exp(mo - mn)
            l_sc[:, sl] = alpha * l_sc[:, sl] + jnp.sum(p, axis=0, keepdims=True)
            m_sc[:, sl] = mn
            alpha_sc[:, sl] = alpha
            p_sc[:, sl] = p.astype(BF16)
        ctb = ckvt_ref[0, :, pl.ds(s0, ks)]
        acc_sc[...] = acc_sc[...] * alpha_sc[...] + _mm(ctb, p_sc[...])
        return carry

    lax.fori_loop(0, nkb, attn_body, 0)

    inv_l = 1.0 / l_sc[...]
    for h in range(N_HEADS):
        sl = slice(h * Q_BLOCK, (h + 1) * Q_BLOCK)
        ol = (acc_sc[:, sl] * inv_l[:, sl]).astype(BF16)
        ot = _mm(wuvt_ref[h], ol)
        o_ref[:, h * hd:(h + 1) * hd] = ot.T


def _dsa(proj, wit, kin, ckvn, ckvt, wuk, wuvt, bsz, s, ks):
    t = bsz * s
    nq = s // Q_BLOCK
    topk = min(TOPK_MAX, s // 4)
    one = pl.Buffered(1)
    kern = functools.partial(_dsa_kernel, ks=ks, topk=topk)
    return pl.pallas_call(
        kern,
        grid=(bsz, nq),
        in_specs=[
            pl.BlockSpec((Q_BLOCK, D_MODEL), lambda b, i: (b * nq + i, OFF_Q // D_MODEL)),
            pl.BlockSpec((Q_BLOCK, IDX_HEADS * IDX_DIM), lambda b, i: (b * nq + i, OFF_QI // 1024)),
            pl.BlockSpec((1, IDX_HEADS, Q_BLOCK), lambda b, i: (b, 0, i)),
            pl.BlockSpec((1, s, IDX_DIM), lambda b, i: (b, 0, 0), pipeline_mode=one),
            pl.BlockSpec((1, s, KV_RANK), lambda b, i: (b, 0, 0), pipeline_mode=one),
            pl.BlockSpec((1, KV_RANK, s), lambda b, i: (b, 0, 0), pipeline_mode=one),
            pl.BlockSpec((N_HEADS, KV_RANK, HEAD_DIM), lambda b, i: (0, 0, 0), pipeline_mode=one),
            pl.BlockSpec((N_HEADS, HEAD_DIM, KV_RANK), lambda b, i: (0, 0, 0), pipeline_mode=one),
        ],
        out_specs=pl.BlockSpec((Q_BLOCK, D_MODEL), lambda b, i: (b * nq + i, 0)),
        out_shape=jax.ShapeDtypeStruct((t, D_MODEL), F32),
        scratch_shapes=[
            pltpu.VMEM((s, Q_BLOCK), jnp.int32),
            pltpu.VMEM((IDX_HEADS, Q_BLOCK, IDX_DIM), BF16),
            pltpu.VMEM((KV_RANK, N_HEADS * Q_BLOCK), BF16),
            pltpu.VMEM((KV_RANK, N_HEADS * Q_BLOCK), F32),
            pltpu.VMEM((1, N_HEADS * Q_BLOCK), F32),
            pltpu.VMEM((1, N_HEADS * Q_BLOCK), F32),
            pltpu.VMEM((1, N_HEADS * Q_BLOCK), F32),
            pltpu.VMEM((ks, N_HEADS * Q_BLOCK), BF16),
        ],
        compiler_params=_cparams(("parallel", "arbitrary")),
        name="dsa",
    )(proj, proj, wit, kin.reshape(bsz, s, IDX_DIM), ckvn.reshape(bsz, s, KV_RANK), ckvt, wuk, wuvt)


def _mixout_kernel(gr_ref, ga_ref, yr_ref, ya_ref, x_ref, wo_ref, x1_ref):
    mixed = jax.nn.sigmoid(gr_ref[...]) * yr_ref[...] + jax.nn.sigmoid(ga_ref[...]) * ya_ref[...]
    x1_ref[...] = x_ref[...] + _mm(mixed.astype(BF16), wo_ref[...])


def _mixout(proj, y_rnn, y_attn, x2d, wo, tm):
    t, d = x2d.shape
    row = lambda i: (i, 0)
    return pl.pallas_call(
        _mixout_kernel,
        grid=(t // tm,),
        in_specs=[
            pl.BlockSpec((tm, d), lambda i: (i, OFF_GR // D_MODEL)),
            pl.BlockSpec((tm, d), lambda i: (i, OFF_GA // D_MODEL)),
            pl.BlockSpec((tm, d), row),
            pl.BlockSpec((tm, d), row),
            pl.BlockSpec((tm, d), row),
            pl.BlockSpec((d, d), lambda i: (0, 0), pipeline_mode=pl.Buffered(1)),
        ],
        out_specs=pl.BlockSpec((tm, d), row),
        out_shape=jax.ShapeDtypeStruct((t, d), F32),
        compiler_params=_cparams(("parallel",)),
        name="mixout",
    )(proj, proj, y_rnn, y_attn, x2d, wo)


def _peerq_kernel(x1_ref, g_ref, wq_ref, k1_ref, k2_ref, xn_ref, s1_ref, s2_ref):
    x = x1_ref[...]
    xn = (x * lax.rsqrt(jnp.mean(x * x, axis=-1, keepdims=True) + EPS)) * g_ref[...]
    xn_ref[...] = xn
    qp = _mm(xn.astype(BF16), wq_ref[...])
    half = PEER_QDIM // 2
    for h in range(PEER_HEADS):
        qa = qp[:, h * PEER_QDIM:h * PEER_QDIM + half].astype(BF16)
        qb = qp[:, h * PEER_QDIM + half:(h + 1) * PEER_QDIM].astype(BF16)
        s1_ref[h] = _mm_nt(k1_ref[...], qa)
        s2_ref[h] = _mm_nt(k2_ref[...], qb)


def _peerq(x1, g, wq, k1, k2, tm):
    t, d = x1.shape
    sc_spec = pl.BlockSpec((PEER_HEADS, PEER_KEYS, tm), lambda i: (0, 0, i))
    sc_shape = jax.ShapeDtypeStruct((PEER_HEADS, PEER_KEYS, t), F32)
    return pl.pallas_call(
        _peerq_kernel,
        grid=(t // tm,),
        in_specs=[
            pl.BlockSpec((tm, d), lambda i: (i, 0)),
            pl.BlockSpec((1, d), lambda i: (0, 0)),
            pl.BlockSpec((d, PEER_HEADS * PEER_QDIM), lambda i: (0, 0), pipeline_mode=pl.Buffered(1)),
            pl.BlockSpec((PEER_KEYS, PEER_QDIM // 2), lambda i: (0, 0)),
            pl.BlockSpec((PEER_KEYS, PEER_QDIM // 2), lambda i: (0, 0)),
        ],
        out_specs=[pl.BlockSpec((tm, d), lambda i: (i, 0)), sc_spec, sc_spec],
        out_shape=[jax.ShapeDtypeStruct((t, d), F32), sc_shape, sc_shape],
        compiler_params=_cparams(("parallel",)),
        name="peerq",
    )(x1, g, wq, k1, k2)


def _top16_rows(v, pos, payload=None):
    vals, poss, pays = [], [], []
    for _ in range(PEER_TOPK):
        m = jnp.max(v, axis=0, keepdims=True)
        p = jnp.min(jnp.where(v == m, pos, jnp.inf), axis=0, keepdims=True)
        hit = pos == p
        vals.append(m)
        poss.append(p)
        if payload is not None:
            pays.append(jnp.sum(jnp.where(hit, payload, 0), axis=0, keepdims=True))
        v = jnp.where(hit, -jnp.inf, v)
    return vals, poss, pays


def _route_kernel(s1_ref, s2_ref, ids_ref, g_ref):
    tg = s1_ref.shape[2]
    k = PEER_TOPK
    key_pos = lax.broadcasted_iota(jnp.int32, (PEER_KEYS, tg), 0).astype(F32)
    b8 = lax.broadcasted_iota(jnp.int32, (8, tg), 0).astype(F32)
    b16 = lax.broadcasted_iota(jnp.int32, (k, tg), 0).astype(F32)
    cand_pos = jnp.concatenate(
        [b16] + [b8 + float(a * k) for a in range(1, 8)] + [(b8 + 8.0) * float(k)], axis=0)
    for h in range(PEER_HEADS):
        v1, p1, _ = _top16_rows(s1_ref[h], key_pos)
        v2, p2, _ = _top16_rows(s2_ref[h], key_pos)
        v1c = jnp.concatenate(v1, axis=0)
        v2c = jnp.concatenate(v2, axis=0)
        i1c = jnp.concatenate(p1, axis=0).astype(jnp.int32) * PEER_KEYS
        i2c = jnp.concatenate(p2, axis=0).astype(jnp.int32)
        cand_s = jnp.concatenate(
            [v1c[0:1] + v2c] + [v1c[a:a + 1] + v2c[:8] for a in range(1, 8)] + [v1c[8:] + v2c[0:1]],
            axis=0)
        cand_i = jnp.concatenate(
            [i1c[0:1] + i2c] + [i1c[a:a + 1] + i2c[:8] for a in range(1, 8)] + [i1c[8:] + i2c[0:1]],
            axis=0)
        top_s, _, experts = _top16_rows(cand_s, cand_pos, payload=cand_i)
        ts = jnp.concatenate(top_s, axis=0)
        e = jnp.exp(ts - ts[0:1, :])
        gate = e / jnp.sum(e, axis=0, keepdims=True)
        ids_ref[0, h * k:(h + 1) * k, :] = jnp.concatenate(experts, axis=0)
        g_ref[0, h * k:(h + 1) * k, :] = gate


def _route(s1t, s2t, tg):
    t = s1t.shape[2]
    ng = t // tg
    sc_spec = pl.BlockSpec((PEER_HEADS, PEER_KEYS, tg), lambda i: (0, 0, i))
    out_spec = pl.BlockSpec((1, PEER_SLOTS, tg), lambda i: (i, 0, 0))
    return pl.pallas_call(
        _route_kernel,
        grid=(ng,),
        in_specs=[sc_spec, sc_spec],
        out_specs=[out_spec, out_spec],
        out_shape=[jax.ShapeDtypeStruct((ng, PEER_SLOTS, tg), jnp.int32),
                   jax.ShapeDtypeStruct((ng, PEER_SLOTS, tg), F32)],
        compiler_params=_cparams(("parallel",)),
        name="route",
    )(s1t, s2t)


N_GATHER_BUFS = 8
GATHER_AHEAD = N_GATHER_BUFS - 1
LANE = 128
N_CHUNKS = D_MODEL // LANE


class _RowIssuer:
    def __init__(self, start_row):
        self._start_row, self._next = start_row, 0

    def issue(self, n):
        for j in range(self._next, min(self._next + n, PEER_SLOTS)):
            self._start_row(j)
        self._next = min(self._next + n, PEER_SLOTS)


def _experts_kernel(ids_ref, idsn_ref, g_ref, xn_ref, x1_ref, gf_ref, uv_ref, y_ref, buf, sem, acc_ref, *, tg):
    i = pl.program_id(0)
    nb = N_GATHER_BUFS
    lane_t = lax.broadcasted_iota(jnp.int32, (PEER_SLOTS, tg), 1)

    def start_row(src_ids, tok, j, slot):
        e = src_ids[0, tok, j]
        prio = j % 2 if isinstance(j, int) else 0
        pltpu.make_async_copy(uv_ref.at[e], buf.at[slot, pl.ds(j, 1), :], sem.at[slot]).start(priority=prio)

    def wait_token(slot):
        pltpu.make_async_copy(buf.at[slot], buf.at[slot], sem.at[slot]).wait()

    @pl.when(i == 0)
    def _():
        for p in range(GATHER_AHEAD):
            def row(j, carry, p=p):
                start_row(ids_ref, p, j, p)
                return carry
            lax.fori_loop(0, PEER_SLOTS, row, 0)

    def u_phase(tok, slot, issuer, per_chunk):
        xrow = xn_ref[pl.ds(tok, 1), :]
        zpart = jnp.zeros((PEER_SLOTS, LANE), F32)
        for c in range(N_CHUNKS):
            sl = slice(c * LANE, (c + 1) * LANE)
            w = buf[slot, :, sl]
            zpart = zpart + pltpu.bitcast(w & jnp.uint32(0xFFFF0000), F32) * xrow[:, sl]
            issuer.issue(per_chunk)
        z = jnp.sum(zpart, axis=-1, keepdims=True)
        gcol = jnp.sum(jnp.where(lane_t == tok, g_ref[0], 0.0), axis=-1, keepdims=True)
        return z, gcol

    def v_phase(tok, slot, act, issuer, per_chunk):
        out = []
        for c in range(N_CHUNKS):
            w = buf[slot, :, c * LANE:(c + 1) * LANE]
            out.append(jnp.sum(act * pltpu.bitcast(w << 16, F32), axis=0, keepdims=True))
            issuer.issue(per_chunk)
        acc_ref[pl.ds(tok, 1), :] = jnp.concatenate(out, axis=-1)

    def activation(z, gcol):
        return (0.5 * z * (1.0 + lax.erf(z * (2.0 ** -0.5)))) * gcol

    def step(tok, r, act, src_ids, ntok, has_next):
        issuer = _RowIssuer(lambda j: start_row(src_ids, ntok, j, (r + GATHER_AHEAD) % nb))
        if has_next:
            wait_token((r + 1) % nb)
            z, gcol = u_phase(tok + 1, (r + 1) % nb, issuer, PEER_SLOTS // (2 * N_CHUNKS))
            v_phase(tok, r, act, issuer, PEER_SLOTS // (2 * N_CHUNKS))
            return activation(z, gcol)
        v_phase(tok, r, act, issuer, PEER_SLOTS // N_CHUNKS)
        return act

    wait_token(0)
    act0 = activation(*u_phase(0, 0, _RowIssuer(lambda j: None), 0))

    def main_body(q, act):
        for r in range(nb):
            tok = q * nb + r
            act = step(tok, r, act, ids_ref, tok + GATHER_AHEAD, True)
        return act

    act = lax.fori_loop(0, tg // nb - 1, main_body, act0)
    for tok in range(tg - nb, tg):
        ntok = tok + GATHER_AHEAD
        src_ids, ntok = (ids_ref, ntok) if ntok < tg else (idsn_ref, ntok - tg)
        act = step(tok, tok % nb, act, src_ids, ntok, tok + 1 < tg)

    @pl.when(i == pl.num_programs(0) - 1)
    def _():
        for p in range(GATHER_AHEAD):
            wait_token(p)

    xf = x1_ref[...] + acc_ref[...]
    y_ref[...] = (xf * lax.rsqrt(jnp.mean(xf * xf, axis=-1, keepdims=True) + EPS)) * gf_ref[...]


def _experts(ids, gates, xn, x1, gf, uv, tg):
    t, d = x1.shape
    ng = t // tg
    assert tg % N_GATHER_BUFS == 0 and tg > GATHER_AHEAD
    ids_tok = jnp.transpose(ids, (0, 2, 1))
    kern = functools.partial(_experts_kernel, tg=tg)
    return pl.pallas_call(
        kern,
        grid=(ng,),
        in_specs=[
            pl.BlockSpec((1, tg, PEER_SLOTS), lambda i: (i, 0, 0), memory_space=pltpu.SMEM),
            pl.BlockSpec((1, tg, PEER_SLOTS), lambda i: (jnp.minimum(i + 1, ng - 1), 0, 0),
                         memory_space=pltpu.SMEM),
            pl.BlockSpec((1, PEER_SLOTS, tg), lambda i: (i, 0, 0)),
            pl.BlockSpec((tg, d), lambda i: (i, 0)),
            pl.BlockSpec((tg, d), lambda i: (i, 0)),
            pl.BlockSpec((1, d), lambda i: (0, 0)),
            pl.BlockSpec(memory_space=pl.ANY),
        ],
        out_specs=pl.BlockSpec((tg, d), lambda i: (i, 0)),
        out_shape=jax.ShapeDtypeStruct((t, d), F32),
        scratch_shapes=[
            pltpu.VMEM((N_GATHER_BUFS, PEER_SLOTS, d), jnp.uint32),
            pltpu.SemaphoreType.DMA((N_GATHER_BUFS,)),
            pltpu.VMEM((tg, d), F32),
        ],
        compiler_params=_cparams(("arbitrary",)),
        name="experts",
    )(ids_tok, ids_tok, gates, xn, x1, gf, uv)


def _regroup_w_in(w):
    d = w.shape[0]
    z = lambda n: jnp.zeros((d, n), w.dtype)
    xr_q = w[:, 0:4096]
    ckv = w[:, 4096:4608]
    qi = w[:, 4608:5632]
    ki = w[:, 5632:5696]
    wi = w[:, 5696:5712]
    gr_ga = w[:, 5712:9808]
    return jnp.concatenate([xr_q, gr_ga, qi, ckv, ki, z(64), wi, z(112)], axis=1).astype(BF16)


def _pack_uv(u, v):
    ub = lax.bitcast_convert_type(u.astype(BF16), jnp.uint16).astype(jnp.uint32)
    vb = lax.bitcast_convert_type(v.astype(BF16), jnp.uint16).astype(jnp.uint32)
    return ((ub << 16) | vb)[:, None, :]


def _layer(x, norm_mix_g, w_in, conv_w, conv_b, rg_wa, rg_ba, rg_wx, rg_bx, rg_lambda,
           kv_norm_g, w_uk, w_uv, idx_ln_g, idx_ln_b, w_o, norm_ffn_g, peer_wq,
           peer_keys1, peer_keys2, peer_u, peer_v, norm_final_g, *,
           tm_in, tn_in, tc_rnn, tp, ks, tm_mix, tm_pq, tg):
    bsz, s, d = x.shape
    t = bsz * s
    x2d = x.reshape(t, d)
    row = lambda a: a.reshape(1, -1)

    proj = _inproj(x2d, row(norm_mix_g), _regroup_w_in(w_in), tm_in, tn_in)
    y_rnn = _rglru(proj, conv_w, row(conv_b), rg_wa.astype(BF16), row(rg_ba),
                   rg_wx.astype(BF16), row(rg_bx), row(rg_lambda), bsz, s, tc_rnn)
    ckvn, ckvt, kin, wit = _prep(proj, row(kv_norm_g), row(idx_ln_g), row(idx_ln_b), bsz, s, tp)
    y_attn = _dsa(proj, wit, kin, ckvn, ckvt, w_uk.astype(BF16),
                  jnp.transpose(w_uv, (0, 2, 1)).astype(BF16), bsz, s, ks)
    x1 = _mixout(proj, y_rnn, y_attn, x2d, w_o.astype(BF16), tm_mix)
    xn, s1t, s2t = _peerq(x1, row(norm_ffn_g), peer_wq.astype(BF16),
                          peer_keys1.astype(BF16), peer_keys2.astype(BF16), tm_pq)
    ids, gates = _route(s1t, s2t, tg)
    y = _experts(ids, gates, xn, x1, row(norm_final_g), _pack_uv(peer_u, peer_v), tg)
    return y.reshape(bsz, s, d)


def kernel(x, norm_mix_g, w_in, conv_w, conv_b, rg_wa, rg_ba, rg_wx, rg_bx, rg_lambda,
           kv_norm_g, w_uk, w_uv, idx_ln_g, idx_ln_b, w_o, norm_ffn_g, peer_wq,
           peer_keys1, peer_keys2, peer_u, peer_v, norm_final_g):
    assert norm_mix_g.shape[0] == 1, "single-layer trunk"
    s = x.shape[1]
    return _layer(
        x, norm_mix_g[0], w_in[0], conv_w[0], conv_b[0], rg_wa[0], rg_ba[0], rg_wx[0], rg_bx[0],
        rg_lambda[0], kv_norm_g[0], w_uk[0], w_uv[0], idx_ln_g[0], idx_ln_b[0], w_o[0],
        norm_ffn_g[0], peer_wq[0], peer_keys1[0], peer_keys2[0], peer_u[0], peer_v[0], norm_final_g,
        tm_in=min(512, s), tn_in=1664, tc_rnn=min(512, s), tp=min(512, s), ks=min(512, s),
        tm_mix=min(256, s), tm_pq=min(256, s), tg=128)
```

```python
import functools

import jax
import jax.numpy as jnp
import numpy as np
from jax import lax
from jax.experimental import pallas as pl
from jax.experimental.pallas import tpu as pltpu

D_MODEL = 2048
RNN_BLOCKS = 16
RNN_BW = D_MODEL // RNN_BLOCKS
CONV_WIDTH = 4
RG_C = 8.0
N_HEADS = 16
HEAD_DIM = 128
KV_RANK = 512
IDX_HEADS = 16
IDX_DIM = 64
TOPK_MAX = 256
Q_BLOCK = 128
PEER_HEADS = 8
PEER_KEYS = 128
PEER_QDIM = 256
PEER_TOPK = 16
PEER_SLOTS = PEER_HEADS * PEER_TOPK
EPS = 1e-6

OFF_XR, OFF_Q, OFF_GR, OFF_GA = 0, 2048, 4096, 6144
OFF_QI, OFF_CKV, OFF_KI, OFF_WI = 8192, 9216, 9728, 9856
N_PROJ = 9984

V7X_VMEM_LIMIT = 56 * 1024 * 1024
COUNT_ROWS = 64
INT_MIN = -(2 ** 31)
NEG_BIG = -1e30

BF16 = jnp.bfloat16
F32 = jnp.float32
NT_DIMS = (((1,), (1,)), ((), ()))


def _mm(a, b):
    return jnp.dot(a, b, preferred_element_type=F32)


def _mm_nt(a, b):
    return lax.dot_general(a, b, NT_DIMS, preferred_element_type=F32)


def _cparams(sem):
    return pltpu.CompilerParams(dimension_semantics=sem, vmem_limit_bytes=V7X_VMEM_LIMIT)


def _inproj_kernel(x_ref, g_ref, w_ref, o_ref, xn_ref):
    @pl.when(pl.program_id(1) == 0)
    def _():
        x = x_ref[...]
        ms = jnp.mean(x * x, axis=-1, keepdims=True)
        xn_ref[...] = ((x * lax.rsqrt(ms + EPS)) * g_ref[...]).astype(BF16)

    o_ref[...] = _mm(xn_ref[...], w_ref[...])


def _inproj(x2d, g, w_r, tm, tn):
    t, d = x2d.shape
    n = w_r.shape[1]
    return pl.pallas_call(
        _inproj_kernel,
        grid=(t // tm, n // tn),
        in_specs=[
            pl.BlockSpec((tm, d), lambda i, j: (i, 0)),
            pl.BlockSpec((1, d), lambda i, j: (0, 0)),
            pl.BlockSpec((d, tn), lambda i, j: (0, j)),
        ],
        out_specs=pl.BlockSpec((tm, tn), lambda i, j: (i, j)),
        out_shape=jax.ShapeDtypeStruct((t, n), F32),
        scratch_shapes=[pltpu.VMEM((tm, d), BF16)],
        compiler_params=_cparams(("parallel", "arbitrary")),
        name="inproj",
    )(x2d, g, w_r)


def _rglru_kernel(x_ref, cw_ref, cb_ref, wa_ref, ba_ref, wx_ref, bx_ref, lam_ref,
                  o_ref, xprev_ref, hprev_ref):
    @pl.when(pl.program_id(2) == 0)
    def _():
        xprev_ref[...] = jnp.zeros_like(xprev_ref)
        hprev_ref[...] = jnp.zeros_like(hprev_ref)

    x = x_ref[...]
    tc = x.shape[0]
    prev8 = xprev_ref[...]
    rows8 = lax.broadcasted_iota(jnp.int32, (8, RNN_BW), 0)
    y = cb_ref[...] + cw_ref[CONV_WIDTH - 1:CONV_WIDTH, :] * x
    for k in range(1, CONV_WIDTH):
        r = pltpu.roll(x, k, 0)
        pr = pltpu.roll(prev8, k, 0)
        top = jnp.where(rows8 < k, pr, r[:8])
        xs = jnp.concatenate([top, r[8:]], axis=0)
        y = y + cw_ref[CONV_WIDTH - 1 - k:CONV_WIDTH - k, :] * xs
    xprev_ref[...] = x[tc - 8:, :]

    xb = y.astype(BF16)
    r_g = jax.nn.sigmoid(_mm(xb, wa_ref[0]) + ba_ref[...])
    i_g = jax.nn.sigmoid(_mm(xb, wx_ref[0]) + bx_ref[...])
    nl = -lam_ref[...]
    sp = jnp.maximum(nl, 0.0) + jnp.log1p(jnp.exp(-jnp.abs(nl)))
    log_a = (-RG_C) * r_g * sp
    a = jnp.exp(log_a)
    u = jnp.sqrt(-jnp.tanh(log_a) * (a * a + 1.0)) * (i_g * y)

    rows = lax.broadcasted_iota(jnp.int32, (tc, RNN_BW), 0)
    d = 1
    while d < tc:
        a_sh = pltpu.roll(a, d, 0)
        u_sh = pltpu.roll(u, d, 0)
        m = rows >= d
        u = jnp.where(m, a * u_sh + u, u)
        a = jnp.where(m, a * a_sh, a)
        d *= 2
    h = u + a * hprev_ref[...]
    o_ref[...] = h
    hprev_ref[...] = h[tc - 1:tc, :]


def _rglru(proj, conv_w, conv_b, wa, ba, wx, bx, lam, bsz, s, tc):
    t = bsz * s
    nc = s // tc
    vec = lambda: pl.BlockSpec((1, RNN_BW), lambda b, n, c: (0, n))
    return pl.pallas_call(
        _rglru_kernel,
        grid=(bsz, RNN_BLOCKS, nc),
        in_specs=[
            pl.BlockSpec((tc, RNN_BW), lambda b, n, c: (b * nc + c, n)),
            pl.BlockSpec((CONV_WIDTH, RNN_BW), lambda b, n, c: (0, n)),
            vec(),
            pl.BlockSpec((1, RNN_BW, RNN_BW), lambda b, n, c: (n, 0, 0)),
            vec(),
            pl.BlockSpec((1, RNN_BW, RNN_BW), lambda b, n, c: (n, 0, 0)),
            vec(),
            vec(),
        ],
        out_specs=pl.BlockSpec((tc, RNN_BW), lambda b, n, c: (b * nc + c, n)),
        out_shape=jax.ShapeDtypeStruct((t, D_MODEL), F32),
        scratch_shapes=[pltpu.VMEM((8, RNN_BW), F32), pltpu.VMEM((1, RNN_BW), F32)],
        compiler_params=_cparams(("parallel", "parallel", "arbitrary")),
        name="rglru",
    )(proj, conv_w, conv_b, wa, ba, wx, bx, lam)


def _prep_kernel(ckv_ref, ki_ref, wi_ref, kvg_ref, lng_ref, lnb_ref,
                 ckvn_ref, ckvt_ref, kin_ref, wit_ref):
    c = ckv_ref[...]
    cn = (c * lax.rsqrt(jnp.mean(c * c, axis=-1, keepdims=True) + EPS)) * kvg_ref[...]
    ckvn_ref[...] = cn.astype(BF16)
    ckvt_ref[0] = cn.T.astype(BF16)
    k = ki_ref[...][:, :IDX_DIM]
    mu = jnp.mean(k, axis=-1, keepdims=True)
    var = jnp.mean(jnp.square(k - mu), axis=-1, keepdims=True)
    kn = (k - mu) * lax.rsqrt(var + EPS)
    kin_ref[...] = (kn * lng_ref[...] + lnb_ref[...]).astype(BF16)
    w = wi_ref[...] * (IDX_HEADS ** -0.5 * IDX_DIM ** -0.5)
    wit_ref[0] = w.T[:IDX_HEADS, :]


def _prep(proj, kvg, lng, lnb, bsz, s, tp):
    t = bsz * s
    nc = s // tp
    return pl.pallas_call(
        _prep_kernel,
        grid=(bsz, nc),
        in_specs=[
            pl.BlockSpec((tp, KV_RANK), lambda b, c: (b * nc + c, OFF_CKV // KV_RANK)),
            pl.BlockSpec((tp, 128), lambda b, c: (b * nc + c, OFF_KI // 128)),
            pl.BlockSpec((tp, 128), lambda b, c: (b * nc + c, OFF_WI // 128)),
            pl.BlockSpec((1, KV_RANK), lambda b, c: (0, 0)),
            pl.BlockSpec((1, IDX_DIM), lambda b, c: (0, 0)),
            pl.BlockSpec((1, IDX_DIM), lambda b, c: (0, 0)),
        ],
        out_specs=[
            pl.BlockSpec((tp, KV_RANK), lambda b, c: (b * nc + c, 0)),
            pl.BlockSpec((1, KV_RANK, tp), lambda b, c: (b, 0, c)),
            pl.BlockSpec((tp, IDX_DIM), lambda b, c: (b * nc + c, 0)),
            pl.BlockSpec((1, IDX_HEADS, tp), lambda b, c: (b, 0, c)),
        ],
        out_shape=[
            jax.ShapeDtypeStruct((t, KV_RANK), BF16),
            jax.ShapeDtypeStruct((bsz, KV_RANK, s), BF16),
            jax.ShapeDtypeStruct((t, IDX_DIM), BF16),
            jax.ShapeDtypeStruct((bsz, IDX_HEADS, s), F32),
        ],
        compiler_params=_cparams(("parallel", "parallel")),
        name="prep",
    )(proj, proj, proj, kvg, lng, lnb)


def _dsa_kernel(q_ref, qi_ref, wit_ref, k_ref, ckv_ref, ckvt_ref, wuk_ref, wuvt_ref, o_ref,
                keys_sc, qit_sc, qlat_sc, acc_sc, m_sc, l_sc, alpha_sc, p_sc, *, ks, topk):
    qb = pl.program_id(1)
    t0 = qb * Q_BLOCK
    nkb = (t0 + Q_BLOCK + ks - 1) // ks
    hd = HEAD_DIM

    for h in range(N_HEADS):
        qh = q_ref[:, h * hd:(h + 1) * hd].astype(BF16)
        qlat_sc[:, h * Q_BLOCK:(h + 1) * Q_BLOCK] = _mm_nt(wuk_ref[h], qh).astype(BF16)
    eye = jnp.where(lax.broadcasted_iota(jnp.int32, (IDX_DIM, IDX_DIM), 0)
                    == lax.broadcasted_iota(jnp.int32, (IDX_DIM, IDX_DIM), 1), 1.0, 0.0).astype(BF16)
    for h in range(IDX_HEADS):
        qih = qi_ref[:, h * IDX_DIM:(h + 1) * IDX_DIM].astype(BF16)
        qit_sc[:, h * Q_BLOCK:(h + 1) * Q_BLOCK] = _mm_nt(eye, qih).astype(BF16)

    lane_q = t0 + lax.broadcasted_iota(jnp.int32, (ks, Q_BLOCK), 1)
    row_i = lax.broadcasted_iota(jnp.int32, (ks, Q_BLOCK), 0)

    def score_body(kb, carry):
        s0 = pl.multiple_of(kb * ks, ks)
        kblk = k_ref[0, pl.ds(s0, ks), :]
        r_all = _mm(kblk, qit_sc[...])
        acc = jnp.zeros((ks, Q_BLOCK), F32)
        for h in range(IDX_HEADS):
            r = r_all[:, h * Q_BLOCK:(h + 1) * Q_BLOCK]
            acc = acc + jnp.maximum(r, 0.0) * wit_ref[0, h:h + 1, :]
        bits = pltpu.bitcast(acc, jnp.int32)
        key = bits ^ ((bits >> 31) & jnp.int32(0x7FFFFFFF))
        key = jnp.where(s0 + row_i <= lane_q, key, jnp.int32(INT_MIN))
        keys_sc[pl.ds(s0, ks), :] = key
        return carry

    lax.fori_loop(0, nkb, score_body, 0)

    def count_ge(cand):
        def body(kb, c):
            s0 = pl.multiple_of(kb * ks, ks)
            blk = keys_sc[pl.ds(s0, ks), :]
            hit = jnp.where(blk >= cand, 1.0, 0.0)
            return c + jnp.sum(hit.reshape(ks // COUNT_ROWS, COUNT_ROWS, Q_BLOCK), axis=0)
        cpart = lax.fori_loop(0, nkb, body, jnp.zeros((COUNT_ROWS, Q_BLOCK), F32))
        return jnp.sum(cpart, axis=0, keepdims=True)

    def bit_body(i, tu):
        bit = lax.shift_left(jnp.int32(1), 31 - i)
        cand_u = tu | bit
        cnt = count_ge(cand_u ^ jnp.int32(INT_MIN))
        return jnp.where(cnt >= float(topk), cand_u, tu)

    tu = lax.fori_loop(0, 32, bit_body, jnp.zeros((1, Q_BLOCK), jnp.int32))
    tsel = jnp.maximum(tu ^ jnp.int32(INT_MIN), jnp.int32(INT_MIN + 1))

    m_sc[...] = jnp.full_like(m_sc, NEG_BIG)
    l_sc[...] = jnp.zeros_like(l_sc)
    acc_sc[...] = jnp.zeros_like(acc_sc)
    scale = HEAD_DIM ** -0.5

    def attn_body(kb, carry):
        s0 = pl.multiple_of(kb * ks, ks)
        cblk = ckv_ref[0, pl.ds(s0, ks), :]
        logt = _mm(cblk, qlat_sc[...]) * scale
        sel = keys_sc[pl.ds(s0, ks), :] >= tsel
        for h in range(N_HEADS):
            sl = slice(h * Q_BLOCK, (h + 1) * Q_BLOCK)
            lg = logt[:, sl]
            mb = jnp.max(jnp.where(sel, lg, NEG_BIG), axis=0, keepdims=True)
            mo = m_sc[:, sl]
            mn = jnp.maximum(mo, mb)
            p = jnp.where(sel, jnp.exp(lg - mn), 0.0)
            alpha = jnp.exp(mo - mn)
            l_sc[:, sl] = alpha * l_sc[:, sl] + jnp.sum(p, axis=0, keepdims=True)
            m_sc[:, sl] = mn
            alpha_sc[:, sl] = alpha
            p_sc[:, sl] = p.astype(BF16)
        ctb = ckvt_ref[0, :, pl.ds(s0, ks)]
        acc_sc[...] = acc_sc[...] * alpha_sc[...] + _mm(ctb, p_sc[...])
        return carry

    lax.fori_loop(0, nkb, attn_body, 0)

    inv_l = 1.0 / l_sc[...]
    for h in range(N_HEADS):
        sl = slice(h * Q_BLOCK, (h + 1) * Q_BLOCK)
        ol = (acc_sc[:, sl] * inv_l[:, sl]).astype(BF16)
        ot = _mm(wuvt_ref[h], ol)
        o_ref[:, h * hd:(h + 1) * hd] = ot.T


def _dsa(proj, wit, kin, ckvn, ckvt, wuk, wuvt, bsz, s, ks):
    t = bsz * s
    nq = s // Q_BLOCK
    topk = min(TOPK_MAX, s // 4)
    one = pl.Buffered(1)
    kern = functools.partial(_dsa_kernel, ks=ks, topk=topk)
    return pl.pallas_call(
        kern,
        grid=(bsz, nq),
        in_specs=[
            pl.BlockSpec((Q_BLOCK, D_MODEL), lambda b, i: (b * nq + i, OFF_Q // D_MODEL)),
            pl.BlockSpec((Q_BLOCK, IDX_HEADS * IDX_DIM), lambda b, i: (b * nq + i, OFF_QI // 1024)),
            pl.BlockSpec((1, IDX_HEADS, Q_BLOCK), lambda b, i: (b, 0, i)),
            pl.BlockSpec((1, s, IDX_DIM), lambda b, i: (b, 0, 0), pipeline_mode=one),
            pl.BlockSpec((1, s, KV_RANK), lambda b, i: (b, 0, 0), pipeline_mode=one),
            pl.BlockSpec((1, KV_RANK, s), lambda b, i: (b, 0, 0), pipeline_mode=one),
            pl.BlockSpec((N_HEADS, KV_RANK, HEAD_DIM), lambda b, i: (0, 0, 0), pipeline_mode=one),
            pl.BlockSpec((N_HEADS, HEAD_DIM, KV_RANK), lambda b, i: (0, 0, 0), pipeline_mode=one),
        ],
        out_specs=pl.BlockSpec((Q_BLOCK, D_MODEL), lambda b, i: (b * nq + i, 0)),
        out_shape=jax.ShapeDtypeStruct((t, D_MODEL), F32),
        scratch_shapes=[
            pltpu.VMEM((s, Q_BLOCK), jnp.int32),
            pltpu.VMEM((IDX_DIM, IDX_HEADS * Q_BLOCK), BF16),
            pltpu.VMEM((KV_RANK, N_HEADS * Q_BLOCK), BF16),
            pltpu.VMEM((KV_RANK, N_HEADS * Q_BLOCK), F32),
            pltpu.VMEM((1, N_HEADS * Q_BLOCK), F32),
            pltpu.VMEM((1, N_HEADS * Q_BLOCK), F32),
            pltpu.VMEM((1, N_HEADS * Q_BLOCK), F32),
            pltpu.VMEM((ks, N_HEADS * Q_BLOCK), BF16),
        ],
        compiler_params=_cparams(("parallel", "arbitrary")),
        name="dsa",
    )(proj, proj, wit, kin.reshape(bsz, s, IDX_DIM), ckvn.reshape(bsz, s, KV_RANK), ckvt, wuk, wuvt)


def _mixout_kernel(gr_ref, ga_ref, yr_ref, ya_ref, x_ref, wo_ref, x1_ref):
    mixed = jax.nn.sigmoid(gr_ref[...]) * yr_ref[...] + jax.nn.sigmoid(ga_ref[...]) * ya_ref[...]
    x1_ref[...] = x_ref[...] + _mm(mixed.astype(BF16), wo_ref[...])


def _mixout(proj, y_rnn, y_attn, x2d, wo, tm):
    t, d = x2d.shape
    row = lambda i: (i, 0)
    return pl.pallas_call(
        _mixout_kernel,
        grid=(t // tm,),
        in_specs=[
            pl.BlockSpec((tm, d), lambda i: (i, OFF_GR // D_MODEL)),
            pl.BlockSpec((tm, d), lambda i: (i, OFF_GA // D_MODEL)),
            pl.BlockSpec((tm, d), row),
            pl.BlockSpec((tm, d), row),
            pl.BlockSpec((tm, d), row),
            pl.BlockSpec((d, d), lambda i: (0, 0), pipeline_mode=pl.Buffered(1)),
        ],
        out_specs=pl.BlockSpec((tm, d), row),
        out_shape=jax.ShapeDtypeStruct((t, d), F32),
        compiler_params=_cparams(("parallel",)),
        name="mixout",
    )(proj, proj, y_rnn, y_attn, x2d, wo)


def _peerq_kernel(x1_ref, g_ref, wq_ref, k1_ref, k2_ref, xn_ref, s1_ref, s2_ref):
    x = x1_ref[...]
    xn = (x * lax.rsqrt(jnp.mean(x * x, axis=-1, keepdims=True) + EPS)) * g_ref[...]
    xn_ref[...] = xn
    qp = _mm(xn.astype(BF16), wq_ref[...])
    half = PEER_QDIM // 2
    for h in range(PEER_HEADS):
        qa = qp[:, h * PEER_QDIM:h * PEER_QDIM + half].astype(BF16)
        qb = qp[:, h * PEER_QDIM + half:(h + 1) * PEER_QDIM].astype(BF16)
        s1_ref[h] = _mm_nt(k1_ref[...], qa)
        s2_ref[h] = _mm_nt(k2_ref[...], qb)


def _peerq(x1, g, wq, k1, k2, tm):
    t, d = x1.shape
    sc_spec = pl.BlockSpec((PEER_HEADS, PEER_KEYS, tm), lambda i: (0, 0, i))
    sc_shape = jax.ShapeDtypeStruct((PEER_HEADS, PEER_KEYS, t), F32)
    return pl.pallas_call(
        _peerq_kernel,
        grid=(t // tm,),
        in_specs=[
            pl.BlockSpec((tm, d), lambda i: (i, 0)),
            pl.BlockSpec((1, d), lambda i: (0, 0)),
            pl.BlockSpec((d, PEER_HEADS * PEER_QDIM), lambda i: (0, 0), pipeline_mode=pl.Buffered(1)),
            pl.BlockSpec((PEER_KEYS, PEER_QDIM // 2), lambda i: (0, 0)),
            pl.BlockSpec((PEER_KEYS, PEER_QDIM // 2), lambda i: (0, 0)),
        ],
        out_specs=[pl.BlockSpec((tm, d), lambda i: (i, 0)), sc_spec, sc_spec],
        out_shape=[jax.ShapeDtypeStruct((t, d), F32), sc_shape, sc_shape],
        compiler_params=_cparams(("parallel",)),
        name="peerq",
    )(x1, g, wq, k1, k2)


def _top16_rows(v, pos, payload=None):
    vals, poss, pays = [], [], []
    for _ in range(PEER_TOPK):
        m = jnp.max(v, axis=0, keepdims=True)
        p = jnp.min(jnp.where(v == m, pos, jnp.inf), axis=0, keepdims=True)
        hit = pos == p
        vals.append(m)
        poss.append(p)
        if payload is not None:
            pays.append(jnp.sum(jnp.where(hit, payload, 0), axis=0, keepdims=True))
        v = jnp.where(hit, -jnp.inf, v)
    return vals, poss, pays


def _route_kernel(s1_ref, s2_ref, ids_ref, g_ref):
    tg = s1_ref.shape[2]
    k = PEER_TOPK
    key_pos = lax.broadcasted_iota(jnp.int32, (PEER_KEYS, tg), 0).astype(F32)
    b8 = lax.broadcasted_iota(jnp.int32, (8, tg), 0).astype(F32)
    b16 = lax.broadcasted_iota(jnp.int32, (k, tg), 0).astype(F32)
    cand_pos = jnp.concatenate(
        [b16] + [b8 + float(a * k) for a in range(1, 8)] + [(b8 + 8.0) * float(k)], axis=0)
    for h in range(PEER_HEADS):
        v1, p1, _ = _top16_rows(s1_ref[h], key_pos)
        v2, p2, _ = _top16_rows(s2_ref[h], key_pos)
        v1c = jnp.concatenate(v1, axis=0)
        v2c = jnp.concatenate(v2, axis=0)
        i1c = jnp.concatenate(p1, axis=0).astype(jnp.int32) * PEER_KEYS
        i2c = jnp.concatenate(p2, axis=0).astype(jnp.int32)
        cand_s = jnp.concatenate(
            [v1c[0:1] + v2c] + [v1c[a:a + 1] + v2c[:8] for a in range(1, 8)] + [v1c[8:] + v2c[0:1]],
            axis=0)
        cand_i = jnp.concatenate(
            [i1c[0:1] + i2c] + [i1c[a:a + 1] + i2c[:8] for a in range(1, 8)] + [i1c[8:] + i2c[0:1]],
            axis=0)
        top_s, _, experts = _top16_rows(cand_s, cand_pos, payload=cand_i)
        ts = jnp.concatenate(top_s, axis=0)
        e = jnp.exp(ts - ts[0:1, :])
        gate = e / jnp.sum(e, axis=0, keepdims=True)
        ids_ref[0, h * k:(h + 1) * k, :] = jnp.concatenate(experts, axis=0)
        g_ref[0, h * k:(h + 1) * k, :] = gate


def _route(s1t, s2t, tg):
    t = s1t.shape[2]
    ng = t // tg
    sc_spec = pl.BlockSpec((PEER_HEADS, PEER_KEYS, tg), lambda i: (0, 0, i))
    out_spec = pl.BlockSpec((1, PEER_SLOTS, tg), lambda i: (i, 0, 0))
    return pl.pallas_call(
        _route_kernel,
        grid=(ng,),
        in_specs=[sc_spec, sc_spec],
        out_specs=[out_spec, out_spec],
        out_shape=[jax.ShapeDtypeStruct((ng, PEER_SLOTS, tg), jnp.int32),
                   jax.ShapeDtypeStruct((ng, PEER_SLOTS, tg), F32)],
        compiler_params=_cparams(("parallel",)),
        name="route",
    )(s1t, s2t)


N_GATHER_BUFS = 8
GATHER_AHEAD = N_GATHER_BUFS - 1
U_PHASE_ISSUE = 1
ISSUE_EVERY = 2
LANE = 128
N_CHUNKS = D_MODEL // LANE


class _RowIssuer:
    def __init__(self, start_row):
        self._start_row, self._next = start_row, 0

    def issue(self, n):
        for j in range(self._next, min(self._next + n, PEER_SLOTS)):
            self._start_row(j)
        self._next = min(self._next + n, PEER_SLOTS)


def _experts_kernel(ids_ref, idsn_ref, g_ref, xn_ref, x1_ref, gf_ref, uv_ref, y_ref, buf, sem, acc_ref, *, tg):
    i = pl.program_id(0)
    nb = N_GATHER_BUFS
    lane_t = lax.broadcasted_iota(jnp.int32, (PEER_SLOTS, tg), 1)

    def start_row(src_ids, tok, j, slot):
        e = src_ids[0, tok, j]
        prio = j % 2 if isinstance(j, int) else 0
        pltpu.make_async_copy(uv_ref.at[e], buf.at[slot, pl.ds(j, 1), :], sem.at[slot]).start(priority=prio)

    def wait_token(slot):
        pltpu.make_async_copy(buf.at[slot], buf.at[slot], sem.at[slot]).wait()

    @pl.when(i == 0)
    def _():
        for p in range(GATHER_AHEAD):
            def row(j, carry, p=p):
                start_row(ids_ref, p, j, p)
                return carry
            lax.fori_loop(0, PEER_SLOTS, row, 0)

    def u_phase(tok, slot, issuer, per_chunk):
        xrow = xn_ref[pl.ds(tok, 1), :]
        zpart = jnp.zeros((PEER_SLOTS, LANE), F32)
        for c in range(N_CHUNKS):
            sl = slice(c * LANE, (c + 1) * LANE)
            w = buf[slot, :, sl]
            zpart = zpart + pltpu.bitcast(w & jnp.uint32(0xFFFF0000), F32) * xrow[:, sl]
            if (c + 1) % ISSUE_EVERY == 0:
                issuer.issue(per_chunk * ISSUE_EVERY)
        z = jnp.sum(zpart, axis=-1, keepdims=True)
        gcol = jnp.sum(jnp.where(lane_t == tok, g_ref[0], 0.0), axis=-1, keepdims=True)
        return z, gcol

    def v_phase(tok, slot, act, issuer, per_chunk):
        out = []
        for c in range(N_CHUNKS):
            w = buf[slot, :, c * LANE:(c + 1) * LANE]
            out.append(jnp.sum(act * pltpu.bitcast(w << 16, F32), axis=0, keepdims=True))
            if (c + 1) % ISSUE_EVERY == 0:
                issuer.issue(per_chunk * ISSUE_EVERY)
        acc_ref[pl.ds(tok, 1), :] = jnp.concatenate(out, axis=-1)

    def activation(z, gcol):
        return (0.5 * z * (1.0 + lax.erf(z * (2.0 ** -0.5)))) * gcol

    def step(tok, r, act, src_ids, ntok, has_next):
        issuer = _RowIssuer(lambda j: start_row(src_ids, ntok, j, (r + GATHER_AHEAD) % nb))
        if has_next:
            wait_token((r + 1) % nb)
            z, gcol = u_phase(tok + 1, (r + 1) % nb, issuer, U_PHASE_ISSUE)
            v_phase(tok, r, act, issuer, PEER_SLOTS // N_CHUNKS - U_PHASE_ISSUE)
            return activation(z, gcol)
        v_phase(tok, r, act, issuer, PEER_SLOTS // N_CHUNKS)
        return act

    wait_token(0)
    act0 = activation(*u_phase(0, 0, _RowIssuer(lambda j: None), 0))

    def main_body(q, act):
        for r in range(nb):
            tok = q * nb + r
            act = step(tok, r, act, ids_ref, tok + GATHER_AHEAD, True)
        return act

    act = lax.fori_loop(0, tg // nb - 1, main_body, act0)
    for tok in range(tg - nb, tg):
        ntok = tok + GATHER_AHEAD
        src_ids, ntok = (ids_ref, ntok) if ntok < tg else (idsn_ref, ntok - tg)
        act = step(tok, tok % nb, act, src_ids, ntok, tok + 1 < tg)

    @pl.when(i == pl.num_programs(0) - 1)
    def _():
        for p in range(GATHER_AHEAD):
            wait_token(p)

    xf = x1_ref[...] + acc_ref[...]
    y_ref[...] = (xf * lax.rsqrt(jnp.mean(xf * xf, axis=-1, keepdims=True) + EPS)) * gf_ref[...]


def _experts(ids, gates, xn, x1, gf, uv, tg):
    t, d = x1.shape
    ng = t // tg
    assert tg % N_GATHER_BUFS == 0 and tg > GATHER_AHEAD
    ids_tok = jnp.transpose(ids, (0, 2, 1))
    kern = functools.partial(_experts_kernel, tg=tg)
    return pl.pallas_call(
        kern,
        grid=(ng,),
        in_specs=[
            pl.BlockSpec((1, tg, PEER_SLOTS), lambda i: (i, 0, 0), memory_space=pltpu.SMEM),
            pl.BlockSpec((1, tg, PEER_SLOTS), lambda i: (jnp.minimum(i + 1, ng - 1), 0, 0),
                         memory_space=pltpu.SMEM),
            pl.BlockSpec((1, PEER_SLOTS, tg), lambda i: (i, 0, 0)),
            pl.BlockSpec((tg, d), lambda i: (i, 0)),
            pl.BlockSpec((tg, d), lambda i: (i, 0)),
            pl.BlockSpec((1, d), lambda i: (0, 0)),
            pl.BlockSpec(memory_space=pl.ANY),
        ],
        out_specs=pl.BlockSpec((tg, d), lambda i: (i, 0)),
        out_shape=jax.ShapeDtypeStruct((t, d), F32),
        scratch_shapes=[
            pltpu.VMEM((N_GATHER_BUFS, PEER_SLOTS, d), jnp.uint32),
            pltpu.SemaphoreType.DMA((N_GATHER_BUFS,)),
            pltpu.VMEM((tg, d), F32),
        ],
        compiler_params=_cparams(("arbitrary",)),
        name="experts",
    )(ids_tok, ids_tok, gates, xn, x1, gf, uv)


def _regroup_w_in(w):
    d = w.shape[0]
    z = lambda n: jnp.zeros((d, n), w.dtype)
    xr_q = w[:, 0:4096]
    ckv = w[:, 4096:4608]
    qi = w[:, 4608:5632]
    ki = w[:, 5632:5696]
    wi = w[:, 5696:5712]
    gr_ga = w[:, 5712:9808]
    return jnp.concatenate([xr_q, gr_ga, qi, ckv, ki, z(64), wi, z(112)], axis=1).astype(BF16)


def _pack_uv(u, v):
    ub = lax.bitcast_convert_type(u.astype(BF16), jnp.uint16).astype(jnp.uint32)
    vb = lax.bitcast_convert_type(v.astype(BF16), jnp.uint16).astype(jnp.uint32)
    return ((ub << 16) | vb)[:, None, :]


def _layer(x, norm_mix_g, w_in, conv_w, conv_b, rg_wa, rg_ba, rg_wx, rg_bx, rg_lambda,
           kv_norm_g, w_uk, w_uv, idx_ln_g, idx_ln_b, w_o, norm_ffn_g, peer_wq,
           peer_keys1, peer_keys2, peer_u, peer_v, norm_final_g, *,
           tm_in, tn_in, tc_rnn, tp, ks, tm_mix, tm_pq, tg):
    bsz, s, d = x.shape
    t = bsz * s
    x2d = x.reshape(t, d)
    row = lambda a: a.reshape(1, -1)

    proj = _inproj(x2d, row(norm_mix_g), _regroup_w_in(w_in), tm_in, tn_in)
    y_rnn = _rglru(proj, conv_w, row(conv_b), rg_wa.astype(BF16), row(rg_ba),
                   rg_wx.astype(BF16), row(rg_bx), row(rg_lambda), bsz, s, tc_rnn)
    ckvn, ckvt, kin, wit = _prep(proj, row(kv_norm_g), row(idx_ln_g), row(idx_ln_b), bsz, s, tp)
    y_attn = _dsa(proj, wit, kin, ckvn, ckvt, w_uk.astype(BF16),
                  jnp.transpose(w_uv, (0, 2, 1)).astype(BF16), bsz, s, ks)
    x1 = _mixout(proj, y_rnn, y_attn, x2d, w_o.astype(BF16), tm_mix)
    xn, s1t, s2t = _peerq(x1, row(norm_ffn_g), peer_wq.astype(BF16),
                          peer_keys1.astype(BF16), peer_keys2.astype(BF16), tm_pq)
    ids, gates = _route(s1t, s2t, tg)
    y = _experts(ids, gates, xn, x1, row(norm_final_g), _pack_uv(peer_u, peer_v), tg)
    return y.reshape(bsz, s, d)


def kernel(x, norm_mix_g, w_in, conv_w, conv_b, rg_wa, rg_ba, rg_wx, rg_bx, rg_lambda,
           kv_norm_g, w_uk, w_uv, idx_ln_g, idx_ln_b, w_o, norm_ffn_g, peer_wq,
           peer_keys1, peer_keys2, peer_u, peer_v, norm_final_g):
    assert norm_mix_g.shape[0] == 1, "single-layer trunk"
    s = x.shape[1]
    return _layer(
        x, norm_mix_g[0], w_in[0], conv_w[0], conv_b[0], rg_wa[0], rg_ba[0], rg_wx[0], rg_bx[0],
        rg_lambda[0], kv_norm_g[0], w_uk[0], w_uv[0], idx_ln_g[0], idx_ln_b[0], w_o[0],
        norm_ffn_g[0], peer_wq[0], peer_keys1[0], peer_keys2[0], peer_u[0], peer_v[0], norm_final_g,
        tm_in=min(1024, s), tn_in=1664, tc_rnn=min(512, s), tp=min(512, s), ks=min(512, s),
        tm_mix=min(256, s), tm_pq=min(256, s), tg=128)
```

```python
import functools

import jax
import jax.numpy as jnp
import numpy as np
from jax import lax
from jax.experimental import pallas as pl
from jax.experimental.pallas import tpu as pltpu

D_MODEL = 2048
RNN_BLOCKS = 16
RNN_BW = D_MODEL // RNN_BLOCKS
CONV_WIDTH = 4
RG_C = 8.0
N_HEADS = 16
HEAD_DIM = 128
KV_RANK = 512
IDX_HEADS = 16
IDX_DIM = 64
TOPK_MAX = 256
Q_BLOCK = 128
PEER_HEADS = 8
PEER_KEYS = 128
PEER_QDIM = 256
PEER_TOPK = 16
PEER_SLOTS = PEER_HEADS * PEER_TOPK
EPS = 1e-6

OFF_XR, OFF_Q, OFF_GR, OFF_GA = 0, 2048, 4096, 6144
OFF_QI, OFF_CKV, OFF_KI, OFF_WI = 8192, 9216, 9728, 9856
N_PROJ = 9984

V7X_VMEM_LIMIT = 56 * 1024 * 1024
COUNT_ROWS16 = 128
HALF = 1 << 15
INT_MIN = -(2 ** 31)
NEG_BIG = -1e30

BF16 = jnp.bfloat16
F32 = jnp.float32
NT_DIMS = (((1,), (1,)), ((), ()))


def _mm(a, b):
    return jnp.dot(a, b, preferred_element_type=F32)


def _mm_nt(a, b):
    return lax.dot_general(a, b, NT_DIMS, preferred_element_type=F32)


def _cparams(sem):
    return pltpu.CompilerParams(dimension_semantics=sem, vmem_limit_bytes=V7X_VMEM_LIMIT)


def _inproj_kernel(x_ref, g_ref, w_ref, o_ref, xn_ref):
    @pl.when(pl.program_id(1) == 0)
    def _():
        x = x_ref[...]
        ms = jnp.mean(x * x, axis=-1, keepdims=True)
        xn_ref[...] = ((x * lax.rsqrt(ms + EPS)) * g_ref[...]).astype(BF16)

    o_ref[...] = _mm(xn_ref[...], w_ref[...])


def _inproj(x2d, g, w_r, tm, tn):
    t, d = x2d.shape
    n = w_r.shape[1]
    return pl.pallas_call(
        _inproj_kernel,
        grid=(t // tm, n // tn),
        in_specs=[
            pl.BlockSpec((tm, d), lambda i, j: (i, 0)),
            pl.BlockSpec((1, d), lambda i, j: (0, 0)),
            pl.BlockSpec((d, tn), lambda i, j: (0, j)),
        ],
        out_specs=pl.BlockSpec((tm, tn), lambda i, j: (i, j)),
        out_shape=jax.ShapeDtypeStruct((t, n), F32),
        scratch_shapes=[pltpu.VMEM((tm, d), BF16)],
        compiler_params=_cparams(("parallel", "arbitrary")),
        name="inproj",
    )(x2d, g, w_r)


def _rglru_kernel(x_ref, cw_ref, cb_ref, wa_ref, ba_ref, wx_ref, bx_ref, lam_ref,
                  o_ref, xprev_ref, hprev_ref):
    @pl.when(pl.program_id(2) == 0)
    def _():
        xprev_ref[...] = jnp.zeros_like(xprev_ref)
        hprev_ref[...] = jnp.zeros_like(hprev_ref)

    x = x_ref[...]
    tc = x.shape[0]
    prev8 = xprev_ref[...]
    rows8 = lax.broadcasted_iota(jnp.int32, (8, RNN_BW), 0)
    y = cb_ref[...] + cw_ref[CONV_WIDTH - 1:CONV_WIDTH, :] * x
    for k in range(1, CONV_WIDTH):
        r = pltpu.roll(x, k, 0)
        pr = pltpu.roll(prev8, k, 0)
        top = jnp.where(rows8 < k, pr, r[:8])
        xs = jnp.concatenate([top, r[8:]], axis=0)
        y = y + cw_ref[CONV_WIDTH - 1 - k:CONV_WIDTH - k, :] * xs
    xprev_ref[...] = x[tc - 8:, :]

    xb = y.astype(BF16)
    r_g = jax.nn.sigmoid(_mm(xb, wa_ref[0]) + ba_ref[...])
    i_g = jax.nn.sigmoid(_mm(xb, wx_ref[0]) + bx_ref[...])
    nl = -lam_ref[...]
    sp = jnp.maximum(nl, 0.0) + jnp.log1p(jnp.exp(-jnp.abs(nl)))
    log_a = (-RG_C) * r_g * sp
    a = jnp.exp(log_a)
    u = jnp.sqrt(-jnp.tanh(log_a) * (a * a + 1.0)) * (i_g * y)

    rows = lax.broadcasted_iota(jnp.int32, (tc, RNN_BW), 0)
    d = 1
    while d < tc:
        a_sh = pltpu.roll(a, d, 0)
        u_sh = pltpu.roll(u, d, 0)
        m = rows >= d
        u = jnp.where(m, a * u_sh + u, u)
        a = jnp.where(m, a * a_sh, a)
        d *= 2
    h = u + a * hprev_ref[...]
    o_ref[...] = h
    hprev_ref[...] = h[tc - 1:tc, :]


def _rglru(proj, conv_w, conv_b, wa, ba, wx, bx, lam, bsz, s, tc):
    t = bsz * s
    nc = s // tc
    vec = lambda: pl.BlockSpec((1, RNN_BW), lambda b, n, c: (0, n))
    return pl.pallas_call(
        _rglru_kernel,
        grid=(bsz, RNN_BLOCKS, nc),
        in_specs=[
            pl.BlockSpec((tc, RNN_BW), lambda b, n, c: (b * nc + c, n)),
            pl.BlockSpec((CONV_WIDTH, RNN_BW), lambda b, n, c: (0, n)),
            vec(),
            pl.BlockSpec((1, RNN_BW, RNN_BW), lambda b, n, c: (n, 0, 0)),
            vec(),
            pl.BlockSpec((1, RNN_BW, RNN_BW), lambda b, n, c: (n, 0, 0)),
            vec(),
            vec(),
        ],
        out_specs=pl.BlockSpec((tc, RNN_BW), lambda b, n, c: (b * nc + c, n)),
        out_shape=jax.ShapeDtypeStruct((t, D_MODEL), F32),
        scratch_shapes=[pltpu.VMEM((8, RNN_BW), F32), pltpu.VMEM((1, RNN_BW), F32)],
        compiler_params=_cparams(("parallel", "parallel", "arbitrary")),
        name="rglru",
    )(proj, conv_w, conv_b, wa, ba, wx, bx, lam)


def _prep_kernel(ckv_ref, ki_ref, wi_ref, kvg_ref, lng_ref, lnb_ref,
                 ckvn_ref, ckvt_ref, kin_ref, wit_ref):
    c = ckv_ref[...]
    cn = (c * lax.rsqrt(jnp.mean(c * c, axis=-1, keepdims=True) + EPS)) * kvg_ref[...]
    ckvn_ref[...] = cn.astype(BF16)
    ckvt_ref[0] = cn.T.astype(BF16)
    k = ki_ref[...][:, :IDX_DIM]
    mu = jnp.mean(k, axis=-1, keepdims=True)
    var = jnp.mean(jnp.square(k - mu), axis=-1, keepdims=True)
    kn = (k - mu) * lax.rsqrt(var + EPS)
    kin_ref[...] = (kn * lng_ref[...] + lnb_ref[...]).astype(BF16)
    w = wi_ref[...] * (IDX_HEADS ** -0.5 * IDX_DIM ** -0.5)
    wit_ref[0] = w.T[:IDX_HEADS, :]


def _prep(proj, kvg, lng, lnb, bsz, s, tp):
    t = bsz * s
    nc = s // tp
    return pl.pallas_call(
        _prep_kernel,
        grid=(bsz, nc),
        in_specs=[
            pl.BlockSpec((tp, KV_RANK), lambda b, c: (b * nc + c, OFF_CKV // KV_RANK)),
            pl.BlockSpec((tp, 128), lambda b, c: (b * nc + c, OFF_KI // 128)),
            pl.BlockSpec((tp, 128), lambda b, c: (b * nc + c, OFF_WI // 128)),
            pl.BlockSpec((1, KV_RANK), lambda b, c: (0, 0)),
            pl.BlockSpec((1, IDX_DIM), lambda b, c: (0, 0)),
            pl.BlockSpec((1, IDX_DIM), lambda b, c: (0, 0)),
        ],
        out_specs=[
            pl.BlockSpec((tp, KV_RANK), lambda b, c: (b * nc + c, 0)),
            pl.BlockSpec((1, KV_RANK, tp), lambda b, c: (b, 0, c)),
            pl.BlockSpec((tp, IDX_DIM), lambda b, c: (b * nc + c, 0)),
            pl.BlockSpec((1, IDX_HEADS, tp), lambda b, c: (b, 0, c)),
        ],
        out_shape=[
            jax.ShapeDtypeStruct((t, KV_RANK), BF16),
            jax.ShapeDtypeStruct((bsz, KV_RANK, s), BF16),
            jax.ShapeDtypeStruct((t, IDX_DIM), BF16),
            jax.ShapeDtypeStruct((bsz, IDX_HEADS, s), F32),
        ],
        compiler_params=_cparams(("parallel", "parallel")),
        name="prep",
    )(proj, proj, proj, kvg, lng, lnb)


def _dsa_kernel(q_ref, qi_ref, wit_ref, k_ref, ckv_ref, ckvt_ref, wuk_ref, wuvt_ref, o_ref,
                keys_sc, hi_sc, lo_sc, qit_sc, qlat_sc, acc_sc, m_sc, l_sc, alpha_sc, p_sc, *, ks, topk):
    qb = pl.program_id(1)
    t0 = qb * Q_BLOCK
    nkb = (t0 + Q_BLOCK + ks - 1) // ks
    hd = HEAD_DIM

    for h in range(N_HEADS):
        qh = q_ref[:, h * hd:(h + 1) * hd].astype(BF16)
        qlat_sc[:, h * Q_BLOCK:(h + 1) * Q_BLOCK] = _mm_nt(wuk_ref[h], qh).astype(BF16)
    eye = jnp.where(lax.broadcasted_iota(jnp.int32, (IDX_DIM, IDX_DIM), 0)
                    == lax.broadcasted_iota(jnp.int32, (IDX_DIM, IDX_DIM), 1), 1.0, 0.0).astype(BF16)
    for h in range(IDX_HEADS):
        qih = qi_ref[:, h * IDX_DIM:(h + 1) * IDX_DIM].astype(BF16)
        qit_sc[:, h * Q_BLOCK:(h + 1) * Q_BLOCK] = _mm_nt(eye, qih).astype(BF16)

    lane_q = t0 + lax.broadcasted_iota(jnp.int32, (ks, Q_BLOCK), 1)
    row_i = lax.broadcasted_iota(jnp.int32, (ks, Q_BLOCK), 0)

    def score_body(kb, carry):
        s0 = pl.multiple_of(kb * ks, ks)
        kblk = k_ref[0, pl.ds(s0, ks), :]
        r_all = _mm(kblk, qit_sc[...])
        acc = jnp.zeros((ks, Q_BLOCK), F32)
        for h in range(IDX_HEADS):
            r = r_all[:, h * Q_BLOCK:(h + 1) * Q_BLOCK]
            acc = acc + jnp.maximum(r, 0.0) * wit_ref[0, h:h + 1, :]
        bits = pltpu.bitcast(acc, jnp.int32)
        key = bits ^ ((bits >> 31) & jnp.int32(0x7FFFFFFF))
        key = jnp.where(s0 + row_i <= lane_q, key, jnp.int32(INT_MIN))
        keys_sc[pl.ds(s0, ks), :] = key
        hi_sc[pl.ds(s0, ks), :] = (key >> 16).astype(jnp.int16)
        lo_sc[pl.ds(s0, ks), :] = ((key & jnp.int32(0xFFFF)) - HALF).astype(jnp.int16)
        return carry

    lax.fori_loop(0, nkb, score_body, 0)

    def count16(ref, cand, strict):
        c16 = cand.astype(jnp.int16)

        def body(kb, c):
            s0 = pl.multiple_of(kb * ks, ks)
            blk = ref[pl.ds(s0, ks), :]
            hit = jnp.where((blk > c16) if strict else (blk >= c16), jnp.int16(1), jnp.int16(0))
            parts = [hit[i * COUNT_ROWS16:(i + 1) * COUNT_ROWS16] for i in range(ks // COUNT_ROWS16)]
            while len(parts) > 1:
                parts = [a + b for a, b in zip(parts[::2], parts[1::2])]
            return c + parts[0]

        cpart = lax.fori_loop(0, nkb, body, jnp.zeros((COUNT_ROWS16, Q_BLOCK), jnp.int16))
        return jnp.sum(cpart.astype(jnp.int32), axis=0, keepdims=True)

    def select16(ref, kneed):
        def bit_body(i, tu):
            cand_u = tu | lax.shift_left(jnp.int32(1), 15 - i)
            cnt = count16(ref, cand_u - HALF, strict=False)
            return jnp.where(cnt >= kneed, cand_u, tu)
        return lax.fori_loop(0, 16, bit_body, jnp.zeros((1, Q_BLOCK), jnp.int32))

    thi = select16(hi_sc, topk) - HALF
    n_above = count16(hi_sc, thi, strict=True)
    thi16 = thi.astype(jnp.int16)

    def tie_body(kb, carry):
        s0 = pl.multiple_of(kb * ks, ks)
        lo_sc[pl.ds(s0, ks), :] = jnp.where(hi_sc[pl.ds(s0, ks), :] == thi16,
                                            lo_sc[pl.ds(s0, ks), :], jnp.int16(-HALF))
        return carry

    lax.fori_loop(0, nkb, tie_body, 0)
    tlo_u = select16(lo_sc, topk - n_above)
    tsel = jnp.maximum(lax.shift_left(thi, 16) + tlo_u, jnp.int32(INT_MIN + 1))

    m_sc[...] = jnp.full_like(m_sc, NEG_BIG)
    l_sc[...] = jnp.zeros_like(l_sc)
    acc_sc[...] = jnp.zeros_like(acc_sc)
    scale = HEAD_DIM ** -0.5

    def attn_body(kb, carry):
        s0 = pl.multiple_of(kb * ks, ks)
        cblk = ckv_ref[0, pl.ds(s0, ks), :]
        logt = _mm(cblk, qlat_sc[...]) * scale
        sel = keys_sc[pl.ds(s0, ks), :] >= tsel
        for h in range(N_HEADS):
            sl = slice(h * Q_BLOCK, (h + 1) * Q_BLOCK)
            lg = logt[:, sl]
            mb = jnp.max(jnp.where(sel, lg, NEG_BIG), axis=0, keepdims=True)
            mo = m_sc[:, sl]
            mn = jnp.maximum(mo, mb)
            p = jnp.where(sel, jnp.exp(lg - mn), 0.0)
            alpha = jnp.exp(mo - mn)
            l_sc[:, sl] = alpha * l_sc[:, sl] + jnp.sum(p, axis=0, keepdims=True)
            m_sc[:, sl] = mn
            alpha_sc[:, sl] = alpha
            p_sc[:, sl] = p.astype(BF16)
        ctb = ckvt_ref[0, :, pl.ds(s0, ks)]
        acc_sc[...] = acc_sc[...] * alpha_sc[...] + _mm(ctb, p_sc[...])
        return carry

    lax.fori_loop(0, nkb, attn_body, 0)

    inv_l = 1.0 / l_sc[...]
    for h in range(N_HEADS):
        sl = slice(h * Q_BLOCK, (h + 1) * Q_BLOCK)
        ol = (acc_sc[:, sl] * inv_l[:, sl]).astype(BF16)
        ot = _mm(wuvt_ref[h], ol)
        o_ref[:, h * hd:(h + 1) * hd] = ot.T


def _dsa(proj, wit, kin, ckvn, ckvt, wuk, wuvt, bsz, s, ks):
    t = bsz * s
    nq = s // Q_BLOCK
    topk = min(TOPK_MAX, s // 4)
    one = pl.Buffered(1)
    kern = functools.partial(_dsa_kernel, ks=ks, topk=topk)
    return pl.pallas_call(
        kern,
        grid=(bsz, nq),
        in_specs=[
            pl.BlockSpec((Q_BLOCK, D_MODEL), lambda b, i: (b * nq + i, OFF_Q // D_MODEL)),
            pl.BlockSpec((Q_BLOCK, IDX_HEADS * IDX_DIM), lambda b, i: (b * nq + i, OFF_QI // 1024)),
            pl.BlockSpec((1, IDX_HEADS, Q_BLOCK), lambda b, i: (b, 0, i)),
            pl.BlockSpec((1, s, IDX_DIM), lambda b, i: (b, 0, 0), pipeline_mode=one),
            pl.BlockSpec((1, s, KV_RANK), lambda b, i: (b, 0, 0), pipeline_mode=one),
            pl.BlockSpec((1, KV_RANK, s), lambda b, i: (b, 0, 0), pipeline_mode=one),
            pl.BlockSpec((N_HEADS, KV_RANK, HEAD_DIM), lambda b, i: (0, 0, 0), pipeline_mode=one),
            pl.BlockSpec((N_HEADS, HEAD_DIM, KV_RANK), lambda b, i: (0, 0, 0), pipeline_mode=one),
        ],
        out_specs=pl.BlockSpec((Q_BLOCK, D_MODEL), lambda b, i: (b * nq + i, 0)),
        out_shape=jax.ShapeDtypeStruct((t, D_MODEL), F32),
        scratch_shapes=[
            pltpu.VMEM((s, Q_BLOCK), jnp.int32),
            pltpu.VMEM((s, Q_BLOCK), jnp.int16),
            pltpu.VMEM((s, Q_BLOCK), jnp.int16),
            pltpu.VMEM((IDX_DIM, IDX_HEADS * Q_BLOCK), BF16),
            pltpu.VMEM((KV_RANK, N_HEADS * Q_BLOCK), BF16),
            pltpu.VMEM((KV_RANK, N_HEADS * Q_BLOCK), F32),
            pltpu.VMEM((1, N_HEADS * Q_BLOCK), F32),
            pltpu.VMEM((1, N_HEADS * Q_BLOCK), F32),
            pltpu.VMEM((1, N_HEADS * Q_BLOCK), F32),
            pltpu.VMEM((ks, N_HEADS * Q_BLOCK), BF16),
        ],
        compiler_params=_cparams(("parallel", "arbitrary")),
        name="dsa",
    )(proj, proj, wit, kin.reshape(bsz, s, IDX_DIM), ckvn.reshape(bsz, s, KV_RANK), ckvt, wuk, wuvt)


def _mixout_kernel(gr_ref, ga_ref, yr_ref, ya_ref, x_ref, wo_ref, x1_ref):
    mixed = jax.nn.sigmoid(gr_ref[...]) * yr_ref[...] + jax.nn.sigmoid(ga_ref[...]) * ya_ref[...]
    x1_ref[...] = x_ref[...] + _mm(mixed.astype(BF16), wo_ref[...])


def _mixout(proj, y_rnn, y_attn, x2d, wo, tm):
    t, d = x2d.shape
    row = lambda i: (i, 0)
    return pl.pallas_call(
        _mixout_kernel,
        grid=(t // tm,),
        in_specs=[
            pl.BlockSpec((tm, d), lambda i: (i, OFF_GR // D_MODEL)),
            pl.BlockSpec((tm, d), lambda i: (i, OFF_GA // D_MODEL)),
            pl.BlockSpec((tm, d), row),
            pl.BlockSpec((tm, d), row),
            pl.BlockSpec((tm, d), row),
            pl.BlockSpec((d, d), lambda i: (0, 0), pipeline_mode=pl.Buffered(1)),
        ],
        out_specs=pl.BlockSpec((tm, d), row),
        out_shape=jax.ShapeDtypeStruct((t, d), F32),
        compiler_params=_cparams(("parallel",)),
        name="mixout",
    )(proj, proj, y_rnn, y_attn, x2d, wo)


def _peerq_kernel(x1_ref, g_ref, wq_ref, k1_ref, k2_ref, xn_ref, s1_ref, s2_ref):
    x = x1_ref[...]
    xn = (x * lax.rsqrt(jnp.mean(x * x, axis=-1, keepdims=True) + EPS)) * g_ref[...]
    xn_ref[...] = xn
    qp = _mm(xn.astype(BF16), wq_ref[...])
    half = PEER_QDIM // 2
    for h in range(PEER_HEADS):
        qa = qp[:, h * PEER_QDIM:h * PEER_QDIM + half].astype(BF16)
        qb = qp[:, h * PEER_QDIM + half:(h + 1) * PEER_QDIM].astype(BF16)
        s1_ref[h] = _mm_nt(k1_ref[...], qa)
        s2_ref[h] = _mm_nt(k2_ref[...], qb)


def _peerq(x1, g, wq, k1, k2, tm):
    t, d = x1.shape
    sc_spec = pl.BlockSpec((PEER_HEADS, PEER_KEYS, tm), lambda i: (0, 0, i))
    sc_shape = jax.ShapeDtypeStruct((PEER_HEADS, PEER_KEYS, t), F32)
    return pl.pallas_call(
        _peerq_kernel,
        grid=(t // tm,),
        in_specs=[
            pl.BlockSpec((tm, d), lambda i: (i, 0)),
            pl.BlockSpec((1, d), lambda i: (0, 0)),
            pl.BlockSpec((d, PEER_HEADS * PEER_QDIM), lambda i: (0, 0), pipeline_mode=pl.Buffered(1)),
            pl.BlockSpec((PEER_KEYS, PEER_QDIM // 2), lambda i: (0, 0)),
            pl.BlockSpec((PEER_KEYS, PEER_QDIM // 2), lambda i: (0, 0)),
        ],
        out_specs=[pl.BlockSpec((tm, d), lambda i: (i, 0)), sc_spec, sc_spec],
        out_shape=[jax.ShapeDtypeStruct((t, d), F32), sc_shape, sc_shape],
        compiler_params=_cparams(("parallel",)),
        name="peerq",
    )(x1, g, wq, k1, k2)


def _top16_rows(v, pos, payload=None):
    vals, poss, pays = [], [], []
    for _ in range(PEER_TOPK):
        m = jnp.max(v, axis=0, keepdims=True)
        p = jnp.min(jnp.where(v == m, pos, jnp.inf), axis=0, keepdims=True)
        hit = pos == p
        vals.append(m)
        poss.append(p)
        if payload is not None:
            pays.append(jnp.sum(jnp.where(hit, payload, 0), axis=0, keepdims=True))
        v = jnp.where(hit, -jnp.inf, v)
    return vals, poss, pays


def _route_kernel(s1_ref, s2_ref, ids_ref, g_ref):
    tg = s1_ref.shape[2]
    k = PEER_TOPK
    key_pos = lax.broadcasted_iota(jnp.int32, (PEER_KEYS, tg), 0).astype(F32)
    b8 = lax.broadcasted_iota(jnp.int32, (8, tg), 0).astype(F32)
    b16 = lax.broadcasted_iota(jnp.int32, (k, tg), 0).astype(F32)
    cand_pos = jnp.concatenate(
        [b16] + [b8 + float(a * k) for a in range(1, 8)] + [(b8 + 8.0) * float(k)], axis=0)
    for h in range(PEER_HEADS):
        v1, p1, _ = _top16_rows(s1_ref[h], key_pos)
        v2, p2, _ = _top16_rows(s2_ref[h], key_pos)
        v1c = jnp.concatenate(v1, axis=0)
        v2c = jnp.concatenate(v2, axis=0)
        i1c = jnp.concatenate(p1, axis=0).astype(jnp.int32) * PEER_KEYS
        i2c = jnp.concatenate(p2, axis=0).astype(jnp.int32)
        cand_s = jnp.concatenate(
            [v1c[0:1] + v2c] + [v1c[a:a + 1] + v2c[:8] for a in range(1, 8)] + [v1c[8:] + v2c[0:1]],
            axis=0)
        cand_i = jnp.concatenate(
            [i1c[0:1] + i2c] + [i1c[a:a + 1] + i2c[:8] for a in range(1, 8)] + [i1c[8:] + i2c[0:1]],
            axis=0)
        top_s, _, experts = _top16_rows(cand_s, cand_pos, payload=cand_i)
        ts = jnp.concatenate(top_s, axis=0)
        e = jnp.exp(ts - ts[0:1, :])
        gate = e / jnp.sum(e, axis=0, keepdims=True)
        ids_ref[0, h * k:(h + 1) * k, :] = jnp.concatenate(experts, axis=0)
        g_ref[0, h * k:(h + 1) * k, :] = gate


def _route(s1t, s2t, tg):
    t = s1t.shape[2]
    ng = t // tg
    sc_spec = pl.BlockSpec((PEER_HEADS, PEER_KEYS, tg), lambda i: (0, 0, i))
    out_spec = pl.BlockSpec((1, PEER_SLOTS, tg), lambda i: (i, 0, 0))
    return pl.pallas_call(
        _route_kernel,
        grid=(ng,),
        in_specs=[sc_spec, sc_spec],
        out_specs=[out_spec, out_spec],
        out_shape=[jax.ShapeDtypeStruct((ng, PEER_SLOTS, tg), jnp.int32),
                   jax.ShapeDtypeStruct((ng, PEER_SLOTS, tg), F32)],
        compiler_params=_cparams(("parallel",)),
        name="route",
    )(s1t, s2t)


N_GATHER_BUFS = 8
GATHER_AHEAD = N_GATHER_BUFS - 1
U_PHASE_ISSUE = 1
ISSUE_EVERY = 2
LANE = 128
N_CHUNKS = D_MODEL // LANE


class _RowIssuer:
    def __init__(self, start_row):
        self._start_row, self._next = start_row, 0

    def issue(self, n):
        for j in range(self._next, min(self._next + n, PEER_SLOTS)):
            self._start_row(j)
        self._next = min(self._next + n, PEER_SLOTS)


def _experts_kernel(ids_ref, idsn_ref, g_ref, xn_ref, x1_ref, gf_ref, uv_ref, y_ref, buf, sem, acc_ref, *, tg):
    i = pl.program_id(0)
    nb = N_GATHER_BUFS
    lane_t = lax.broadcasted_iota(jnp.int32, (PEER_SLOTS, tg), 1)

    def start_row(src_ids, tok, j, slot):
        e = src_ids[0, tok, j]
        prio = j % 2 if isinstance(j, int) else 0
        pltpu.make_async_copy(uv_ref.at[e], buf.at[slot, pl.ds(j, 1), :], sem.at[slot]).start(priority=prio)

    def wait_token(slot):
        pltpu.make_async_copy(buf.at[slot], buf.at[slot], sem.at[slot]).wait()

    @pl.when(i == 0)
    def _():
        for p in range(GATHER_AHEAD):
            def row(j, carry, p=p):
                start_row(ids_ref, p, j, p)
                return carry
            lax.fori_loop(0, PEER_SLOTS, row, 0)

    def u_phase(tok, slot, issuer, per_chunk):
        xrow = xn_ref[pl.ds(tok, 1), :]
        zpart = jnp.zeros((PEER_SLOTS, LANE), F32)
        for c in range(N_CHUNKS):
            sl = slice(c * LANE, (c + 1) * LANE)
            w = buf[slot, :, sl]
            zpart = zpart + pltpu.bitcast(w & jnp.uint32(0xFFFF0000), F32) * xrow[:, sl]
            if (c + 1) % ISSUE_EVERY == 0:
                issuer.issue(per_chunk * ISSUE_EVERY)
        z = jnp.sum(zpart, axis=-1, keepdims=True)
        gcol = jnp.sum(jnp.where(lane_t == tok, g_ref[0], 0.0), axis=-1, keepdims=True)
        return z, gcol

    def v_phase(tok, slot, act, issuer, per_chunk):
        out = []
        for c in range(N_CHUNKS):
            w = buf[slot, :, c * LANE:(c + 1) * LANE]
            out.append(jnp.sum(act * pltpu.bitcast(w << 16, F32), axis=0, keepdims=True))
            if (c + 1) % ISSUE_EVERY == 0:
                issuer.issue(per_chunk * ISSUE_EVERY)
        acc_ref[pl.ds(tok, 1), :] = jnp.concatenate(out, axis=-1)

    def activation(z, gcol):
        return (0.5 * z * (1.0 + lax.erf(z * (2.0 ** -0.5)))) * gcol

    def step(tok, r, act, src_ids, ntok, has_next):
        issuer = _RowIssuer(lambda j: start_row(src_ids, ntok, j, (r + GATHER_AHEAD) % nb))
        if has_next:
            wait_token((r + 1) % nb)
            z, gcol = u_phase(tok + 1, (r + 1) % nb, issuer, U_PHASE_ISSUE)
            v_phase(tok, r, act, issuer, PEER_SLOTS // N_CHUNKS - U_PHASE_ISSUE)
            return activation(z, gcol)
        v_phase(tok, r, act, issuer, PEER_SLOTS // N_CHUNKS)
        return act

    wait_token(0)
    act0 = activation(*u_phase(0, 0, _RowIssuer(lambda j: None), 0))

    def main_body(q, act):
        for r in range(nb):
            tok = q * nb + r
            act = step(tok, r, act, ids_ref, tok + GATHER_AHEAD, True)
        return act

    act = lax.fori_loop(0, tg // nb - 1, main_body, act0)
    for tok in range(tg - nb, tg):
        ntok = tok + GATHER_AHEAD
        src_ids, ntok = (ids_ref, ntok) if ntok < tg else (idsn_ref, ntok - tg)
        act = step(tok, tok % nb, act, src_ids, ntok, tok + 1 < tg)

    @pl.when(i == pl.num_programs(0) - 1)
    def _():
        for p in range(GATHER_AHEAD):
            wait_token(p)

    xf = x1_ref[...] + acc_ref[...]
    y_ref[...] = (xf * lax.rsqrt(jnp.mean(xf * xf, axis=-1, keepdims=True) + EPS)) * gf_ref[...]


def _experts(ids, gates, xn, x1, gf, uv, tg):
    t, d = x1.shape
    ng = t // tg
    assert tg % N_GATHER_BUFS == 0 and tg > GATHER_AHEAD
    ids_tok = jnp.transpose(ids, (0, 2, 1))
    kern = functools.partial(_experts_kernel, tg=tg)
    return pl.pallas_call(
        kern,
        grid=(ng,),
        in_specs=[
            pl.BlockSpec((1, tg, PEER_SLOTS), lambda i: (i, 0, 0), memory_space=pltpu.SMEM),
            pl.BlockSpec((1, tg, PEER_SLOTS), lambda i: (jnp.minimum(i + 1, ng - 1), 0, 0),
                         memory_space=pltpu.SMEM),
            pl.BlockSpec((1, PEER_SLOTS, tg), lambda i: (i, 0, 0)),
            pl.BlockSpec((tg, d), lambda i: (i, 0)),
            pl.BlockSpec((tg, d), lambda i: (i, 0)),
            pl.BlockSpec((1, d), lambda i: (0, 0)),
            pl.BlockSpec(memory_space=pl.ANY),
        ],
        out_specs=pl.BlockSpec((tg, d), lambda i: (i, 0)),
        out_shape=jax.ShapeDtypeStruct((t, d), F32),
        scratch_shapes=[
            pltpu.VMEM((N_GATHER_BUFS, PEER_SLOTS, d), jnp.uint32),
            pltpu.SemaphoreType.DMA((N_GATHER_BUFS,)),
            pltpu.VMEM((tg, d), F32),
        ],
        compiler_params=_cparams(("arbitrary",)),
        name="experts",
    )(ids_tok, ids_tok, gates, xn, x1, gf, uv)


def _regroup_w_in(w):
    d = w.shape[0]
    z = lambda n: jnp.zeros((d, n), w.dtype)
    xr_q = w[:, 0:4096]
    ckv = w[:, 4096:4608]
    qi = w[:, 4608:5632]
    ki = w[:, 5632:5696]
    wi = w[:, 5696:5712]
    gr_ga = w[:, 5712:9808]
    return jnp.concatenate([xr_q, gr_ga, qi, ckv, ki, z(64), wi, z(112)], axis=1).astype(BF16)


def _pack_uv(u, v):
    ub = lax.bitcast_convert_type(u.astype(BF16), jnp.uint16).astype(jnp.uint32)
    vb = lax.bitcast_convert_type(v.astype(BF16), jnp.uint16).astype(jnp.uint32)
    return ((ub << 16) | vb)[:, None, :]


def _layer(x, norm_mix_g, w_in, conv_w, conv_b, rg_wa, rg_ba, rg_wx, rg_bx, rg_lambda,
           kv_norm_g, w_uk, w_uv, idx_ln_g, idx_ln_b, w_o, norm_ffn_g, peer_wq,
           peer_keys1, peer_keys2, peer_u, peer_v, norm_final_g, *,
           tm_in, tn_in, tc_rnn, tp, ks, tm_mix, tm_pq, tg):
    bsz, s, d = x.shape
    t = bsz * s
    x2d = x.reshape(t, d)
    row = lambda a: a.reshape(1, -1)

    proj = _inproj(x2d, row(norm_mix_g), _regroup_w_in(w_in), tm_in, tn_in)
    y_rnn = _rglru(proj, conv_w, row(conv_b), rg_wa.astype(BF16), row(rg_ba),
                   rg_wx.astype(BF16), row(rg_bx), row(rg_lambda), bsz, s, tc_rnn)
    ckvn, ckvt, kin, wit = _prep(proj, row(kv_norm_g), row(idx_ln_g), row(idx_ln_b), bsz, s, tp)
    y_attn = _dsa(proj, wit, kin, ckvn, ckvt, w_uk.astype(BF16),
                  jnp.transpose(w_uv, (0, 2, 1)).astype(BF16), bsz, s, ks)
    x1 = _mixout(proj, y_rnn, y_attn, x2d, w_o.astype(BF16), tm_mix)
    xn, s1t, s2t = _peerq(x1, row(norm_ffn_g), peer_wq.astype(BF16),
                          peer_keys1.astype(BF16), peer_keys2.astype(BF16), tm_pq)
    ids, gates = _route(s1t, s2t, tg)
    y = _experts(ids, gates, xn, x1, row(norm_final_g), _pack_uv(peer_u, peer_v), tg)
    return y.reshape(bsz, s, d)


def kernel(x, norm_mix_g, w_in, conv_w, conv_b, rg_wa, rg_ba, rg_wx, rg_bx, rg_lambda,
           kv_norm_g, w_uk, w_uv, idx_ln_g, idx_ln_b, w_o, norm_ffn_g, peer_wq,
           peer_keys1, peer_keys2, peer_u, peer_v, norm_final_g):
    assert norm_mix_g.shape[0] == 1, "single-layer trunk"
    s = x.shape[1]
    return _layer(
        x, norm_mix_g[0], w_in[0], conv_w[0], conv_b[0], rg_wa[0], rg_ba[0], rg_wx[0], rg_bx[0],
        rg_lambda[0], kv_norm_g[0], w_uk[0], w_uv[0], idx_ln_g[0], idx_ln_b[0], w_o[0],
        norm_ffn_g[0], peer_wq[0], peer_keys1[0], peer_keys2[0], peer_u[0], peer_v[0], norm_final_g,
        tm_in=min(1024, s), tn_in=1664, tc_rnn=min(512, s), tp=min(512, s), ks=min(512, s),
        tm_mix=min(256, s), tm_pq=min(256, s), tg=128)
```

```python
import functools

import jax
import jax.numpy as jnp
import numpy as np
from jax import lax
from jax.experimental import pallas as pl
from jax.experimental.pallas import tpu as pltpu

D_MODEL = 2048
RNN_BLOCKS = 16
RNN_BW = D_MODEL // RNN_BLOCKS
CONV_WIDTH = 4
RG_C = 8.0
N_HEADS = 16
HEAD_DIM = 128
KV_RANK = 512
IDX_HEADS = 16
IDX_DIM = 64
TOPK_MAX = 256
Q_BLOCK = 128
PEER_HEADS = 8
PEER_KEYS = 128
PEER_QDIM = 256
PEER_TOPK = 16
PEER_SLOTS = PEER_HEADS * PEER_TOPK
EPS = 1e-6

OFF_XR, OFF_Q, OFF_GR, OFF_GA = 0, 2048, 4096, 6144
OFF_QI, OFF_CKV, OFF_KI, OFF_WI = 8192, 9216, 9728, 9856
N_PROJ = 9984

V7X_VMEM_LIMIT = 56 * 1024 * 1024
COUNT_ROWS = 64
INT_MIN = -(2 ** 31)
NEG_BIG = -1e30

BF16 = jnp.bfloat16
F32 = jnp.float32
NT_DIMS = (((1,), (1,)), ((), ()))


def _mm(a, b):
    return jnp.dot(a, b, preferred_element_type=F32)


def _mm_nt(a, b):
    return lax.dot_general(a, b, NT_DIMS, preferred_element_type=F32)


def _cparams(sem):
    return pltpu.CompilerParams(dimension_semantics=sem, vmem_limit_bytes=V7X_VMEM_LIMIT)


def _inproj_kernel(x_ref, g_ref, w_ref, o_ref, xn_ref):
    @pl.when(pl.program_id(1) == 0)
    def _():
        x = x_ref[...]
        ms = jnp.mean(x * x, axis=-1, keepdims=True)
        xn_ref[...] = ((x * lax.rsqrt(ms + EPS)) * g_ref[...]).astype(BF16)

    o_ref[...] = _mm(xn_ref[...], w_ref[...])


def _inproj(x2d, g, w_r, tm, tn):
    t, d = x2d.shape
    n = w_r.shape[1]
    return pl.pallas_call(
        _inproj_kernel,
        grid=(t // tm, n // tn),
        in_specs=[
            pl.BlockSpec((tm, d), lambda i, j: (i, 0)),
            pl.BlockSpec((1, d), lambda i, j: (0, 0)),
            pl.BlockSpec((d, tn), lambda i, j: (0, j)),
        ],
        out_specs=pl.BlockSpec((tm, tn), lambda i, j: (i, j)),
        out_shape=jax.ShapeDtypeStruct((t, n), F32),
        scratch_shapes=[pltpu.VMEM((tm, d), BF16)],
        compiler_params=_cparams(("parallel", "arbitrary")),
        name="inproj",
    )(x2d, g, w_r)


def _rglru_kernel(x_ref, cw_ref, cb_ref, wa_ref, ba_ref, wx_ref, bx_ref, lam_ref,
                  o_ref, xprev_ref, hprev_ref):
    @pl.when(pl.program_id(2) == 0)
    def _():
        xprev_ref[...] = jnp.zeros_like(xprev_ref)
        hprev_ref[...] = jnp.zeros_like(hprev_ref)

    x = x_ref[...]
    tc = x.shape[0]
    prev8 = xprev_ref[...]
    rows8 = lax.broadcasted_iota(jnp.int32, (8, RNN_BW), 0)
    y = cb_ref[...] + cw_ref[CONV_WIDTH - 1:CONV_WIDTH, :] * x
    for k in range(1, CONV_WIDTH):
        r = pltpu.roll(x, k, 0)
        pr = pltpu.roll(prev8, k, 0)
        top = jnp.where(rows8 < k, pr, r[:8])
        xs = jnp.concatenate([top, r[8:]], axis=0)
        y = y + cw_ref[CONV_WIDTH - 1 - k:CONV_WIDTH - k, :] * xs
    xprev_ref[...] = x[tc - 8:, :]

    xb = y.astype(BF16)
    r_g = jax.nn.sigmoid(_mm(xb, wa_ref[0]) + ba_ref[...])
    i_g = jax.nn.sigmoid(_mm(xb, wx_ref[0]) + bx_ref[...])
    nl = -lam_ref[...]
    sp = jnp.maximum(nl, 0.0) + jnp.log1p(jnp.exp(-jnp.abs(nl)))
    log_a = (-RG_C) * r_g * sp
    a = jnp.exp(log_a)
    u = jnp.sqrt(-jnp.tanh(log_a) * (a * a + 1.0)) * (i_g * y)

    rows = lax.broadcasted_iota(jnp.int32, (tc, RNN_BW), 0)
    d = 1
    while d < tc:
        a_sh = pltpu.roll(a, d, 0)
        u_sh = pltpu.roll(u, d, 0)
        m = rows >= d
        u = jnp.where(m, a * u_sh + u, u)
        a = jnp.where(m, a * a_sh, a)
        d *= 2
    h = u + a * hprev_ref[...]
    o_ref[...] = h
    hprev_ref[...] = h[tc - 1:tc, :]


def _rglru(proj, conv_w, conv_b, wa, ba, wx, bx, lam, bsz, s, tc):
    t = bsz * s
    nc = s // tc
    vec = lambda: pl.BlockSpec((1, RNN_BW), lambda b, n, c: (0, n))
    return pl.pallas_call(
        _rglru_kernel,
        grid=(bsz, RNN_BLOCKS, nc),
        in_specs=[
            pl.BlockSpec((tc, RNN_BW), lambda b, n, c: (b * nc + c, n)),
            pl.BlockSpec((CONV_WIDTH, RNN_BW), lambda b, n, c: (0, n)),
            vec(),
            pl.BlockSpec((1, RNN_BW, RNN_BW), lambda b, n, c: (n, 0, 0)),
            vec(),
            pl.BlockSpec((1, RNN_BW, RNN_BW), lambda b, n, c: (n, 0, 0)),
            vec(),
            vec(),
        ],
        out_specs=pl.BlockSpec((tc, RNN_BW), lambda b, n, c: (b * nc + c, n)),
        out_shape=jax.ShapeDtypeStruct((t, D_MODEL), F32),
        scratch_shapes=[pltpu.VMEM((8, RNN_BW), F32), pltpu.VMEM((1, RNN_BW), F32)],
        compiler_params=_cparams(("parallel", "parallel", "arbitrary")),
        name="rglru",
    )(proj, conv_w, conv_b, wa, ba, wx, bx, lam)


def _prep_kernel(ckv_ref, ki_ref, wi_ref, kvg_ref, lng_ref, lnb_ref,
                 ckvn_ref, ckvt_ref, kin_ref, wit_ref):
    c = ckv_ref[...]
    cn = (c * lax.rsqrt(jnp.mean(c * c, axis=-1, keepdims=True) + EPS)) * kvg_ref[...]
    ckvn_ref[...] = cn.astype(BF16)
    ckvt_ref[0] = cn.T.astype(BF16)
    k = ki_ref[...][:, :IDX_DIM]
    mu = jnp.mean(k, axis=-1, keepdims=True)
    var = jnp.mean(jnp.square(k - mu), axis=-1, keepdims=True)
    kn = (k - mu) * lax.rsqrt(var + EPS)
    kin_ref[...] = (kn * lng_ref[...] + lnb_ref[...]).astype(BF16)
    w = wi_ref[...] * (IDX_HEADS ** -0.5 * IDX_DIM ** -0.5)
    wit_ref[0] = w.T[:IDX_HEADS, :]


def _prep(proj, kvg, lng, lnb, bsz, s, tp):
    t = bsz * s
    nc = s // tp
    return pl.pallas_call(
        _prep_kernel,
        grid=(bsz, nc),
        in_specs=[
            pl.BlockSpec((tp, KV_RANK), lambda b, c: (b * nc + c, OFF_CKV // KV_RANK)),
            pl.BlockSpec((tp, 128), lambda b, c: (b * nc + c, OFF_KI // 128)),
            pl.BlockSpec((tp, 128), lambda b, c: (b * nc + c, OFF_WI // 128)),
            pl.BlockSpec((1, KV_RANK), lambda b, c: (0, 0)),
            pl.BlockSpec((1, IDX_DIM), lambda b, c: (0, 0)),
            pl.BlockSpec((1, IDX_DIM), lambda b, c: (0, 0)),
        ],
        out_specs=[
            pl.BlockSpec((tp, KV_RANK), lambda b, c: (b * nc + c, 0)),
            pl.BlockSpec((1, KV_RANK, tp), lambda b, c: (b, 0, c)),
            pl.BlockSpec((tp, IDX_DIM), lambda b, c: (b * nc + c, 0)),
            pl.BlockSpec((1, IDX_HEADS, tp), lambda b, c: (b, 0, c)),
        ],
        out_shape=[
            jax.ShapeDtypeStruct((t, KV_RANK), BF16),
            jax.ShapeDtypeStruct((bsz, KV_RANK, s), BF16),
            jax.ShapeDtypeStruct((t, IDX_DIM), BF16),
            jax.ShapeDtypeStruct((bsz, IDX_HEADS, s), F32),
        ],
        compiler_params=_cparams(("parallel", "parallel")),
        name="prep",
    )(proj, proj, proj, kvg, lng, lnb)


def _dsa_kernel(q_ref, qi_ref, wit_ref, k_ref, ckv_ref, ckvt_ref, wuk_ref, wuvt_ref, o_ref,
                keys_sc, qit_sc, qlat_sc, acc_sc, m_sc, l_sc, alpha_sc, p_sc, *, ks, topk):
    qb = pl.program_id(1)
    t0 = qb * Q_BLOCK
    nkb = (t0 + Q_BLOCK + ks - 1) // ks
    hd = HEAD_DIM

    for h in range(N_HEADS):
        qh = q_ref[:, h * hd:(h + 1) * hd].astype(BF16)
        qlat_sc[:, h * Q_BLOCK:(h + 1) * Q_BLOCK] = (_mm_nt(wuk_ref[h], qh) * HEAD_DIM ** -0.5).astype(BF16)
    eye = jnp.where(lax.broadcasted_iota(jnp.int32, (IDX_DIM, IDX_DIM), 0)
                    == lax.broadcasted_iota(jnp.int32, (IDX_DIM, IDX_DIM), 1), 1.0, 0.0).astype(BF16)
    for h in range(IDX_HEADS):
        qih = qi_ref[:, h * IDX_DIM:(h + 1) * IDX_DIM].astype(BF16)
        qit_sc[:, h * Q_BLOCK:(h + 1) * Q_BLOCK] = _mm_nt(eye, qih).astype(BF16)

    lane_q = t0 + lax.broadcasted_iota(jnp.int32, (ks, Q_BLOCK), 1)
    row_i = lax.broadcasted_iota(jnp.int32, (ks, Q_BLOCK), 0)

    def score_body(kb, carry):
        s0 = pl.multiple_of(kb * ks, ks)
        kblk = k_ref[0, pl.ds(s0, ks), :]
        r_all = _mm(kblk, qit_sc[...])
        acc = jnp.zeros((ks, Q_BLOCK), F32)
        for h in range(IDX_HEADS):
            r = r_all[:, h * Q_BLOCK:(h + 1) * Q_BLOCK]
            acc = acc + jnp.maximum(r, 0.0) * wit_ref[0, h:h + 1, :]
        bits = pltpu.bitcast(acc, jnp.int32)
        key = bits ^ ((bits >> 31) & jnp.int32(0x7FFFFFFF))
        key = jnp.where(s0 + row_i <= lane_q, key, jnp.int32(INT_MIN))
        keys_sc[pl.ds(s0, ks), :] = key
        return carry

    lax.fori_loop(0, nkb, score_body, 0)

    def count_ge(cand):
        def body(kb, c):
            s0 = pl.multiple_of(kb * ks, ks)
            blk = keys_sc[pl.ds(s0, ks), :]
            hit = jnp.where(blk >= cand, 1.0, 0.0)
            return c + jnp.sum(hit.reshape(ks // COUNT_ROWS, COUNT_ROWS, Q_BLOCK), axis=0)
        cpart = lax.fori_loop(0, nkb, body, jnp.zeros((COUNT_ROWS, Q_BLOCK), F32))
        return jnp.sum(cpart, axis=0, keepdims=True)

    def bit_body(i, tu):
        bit = lax.shift_left(jnp.int32(1), 31 - i)
        cand_u = tu | bit
        cnt = count_ge(cand_u ^ jnp.int32(INT_MIN))
        return jnp.where(cnt >= float(topk), cand_u, tu)

    tu = lax.fori_loop(0, 32, bit_body, jnp.zeros((1, Q_BLOCK), jnp.int32))
    tsel = jnp.maximum(tu ^ jnp.int32(INT_MIN), jnp.int32(INT_MIN + 1))

    m_sc[...] = jnp.full_like(m_sc, NEG_BIG)
    l_sc[...] = jnp.zeros_like(l_sc)
    acc_sc[...] = jnp.zeros_like(acc_sc)

    def attn_body(kb, carry):
        s0 = pl.multiple_of(kb * ks, ks)
        cblk = ckv_ref[0, pl.ds(s0, ks), :]
        logt = _mm(cblk, qlat_sc[...])
        sel = keys_sc[pl.ds(s0, ks), :] >= tsel
        for h in range(N_HEADS):
            sl = slice(h * Q_BLOCK, (h + 1) * Q_BLOCK)
            lg = jnp.where(sel, logt[:, sl], NEG_BIG)
            mb = jnp.max(lg, axis=0, keepdims=True)
            mo = m_sc[:, sl]
            mn = jnp.maximum(mo, mb)
            p = jnp.exp(lg - mn)
            alpha = jnp.exp(mo - mn)
            l_sc[:, sl] = alpha * l_sc[:, sl] + jnp.sum(p, axis=0, keepdims=True)
            m_sc[:, sl] = mn
            alpha_sc[:, sl] = alpha
            p_sc[:, sl] = p.astype(BF16)
        ctb = ckvt_ref[0, :, pl.ds(s0, ks)]
        acc_sc[...] = acc_sc[...] * alpha_sc[...] + _mm(ctb, p_sc[...])
        return carry

    lax.fori_loop(0, nkb, attn_body, 0)

    inv_l = 1.0 / l_sc[...]
    for h in range(N_HEADS):
        sl = slice(h * Q_BLOCK, (h + 1) * Q_BLOCK)
        ol = (acc_sc[:, sl] * inv_l[:, sl]).astype(BF16)
        ot = _mm(wuvt_ref[h], ol)
        o_ref[:, h * hd:(h + 1) * hd] = ot.T


def _dsa(proj, wit, kin, ckvn, ckvt, wuk, wuvt, bsz, s, ks):
    t = bsz * s
    nq = s // Q_BLOCK
    topk = min(TOPK_MAX, s // 4)
    one = pl.Buffered(1)
    kern = functools.partial(_dsa_kernel, ks=ks, topk=topk)
    return pl.pallas_call(
        kern,
        grid=(bsz, nq),
        in_specs=[
            pl.BlockSpec((Q_BLOCK, D_MODEL), lambda b, i: (b * nq + i, OFF_Q // D_MODEL)),
            pl.BlockSpec((Q_BLOCK, IDX_HEADS * IDX_DIM), lambda b, i: (b * nq + i, OFF_QI // 1024)),
            pl.BlockSpec((1, IDX_HEADS, Q_BLOCK), lambda b, i: (b, 0, i)),
            pl.BlockSpec((1, s, IDX_DIM), lambda b, i: (b, 0, 0), pipeline_mode=one),
            pl.BlockSpec((1, s, KV_RANK), lambda b, i: (b, 0, 0), pipeline_mode=one),
            pl.BlockSpec((1, KV_RANK, s), lambda b, i: (b, 0, 0), pipeline_mode=one),
            pl.BlockSpec((N_HEADS, KV_RANK, HEAD_DIM), lambda b, i: (0, 0, 0), pipeline_mode=one),
            pl.BlockSpec((N_HEADS, HEAD_DIM, KV_RANK), lambda b, i: (0, 0, 0), pipeline_mode=one),
        ],
        out_specs=pl.BlockSpec((Q_BLOCK, D_MODEL), lambda b, i: (b * nq + i, 0)),
        out_shape=jax.ShapeDtypeStruct((t, D_MODEL), F32),
        scratch_shapes=[
            pltpu.VMEM((s, Q_BLOCK), jnp.int32),
            pltpu.VMEM((IDX_DIM, IDX_HEADS * Q_BLOCK), BF16),
            pltpu.VMEM((KV_RANK, N_HEADS * Q_BLOCK), BF16),
            pltpu.VMEM((KV_RANK, N_HEADS * Q_BLOCK), F32),
            pltpu.VMEM((1, N_HEADS * Q_BLOCK), F32),
            pltpu.VMEM((1, N_HEADS * Q_BLOCK), F32),
            pltpu.VMEM((1, N_HEADS * Q_BLOCK), F32),
            pltpu.VMEM((ks, N_HEADS * Q_BLOCK), BF16),
        ],
        compiler_params=_cparams(("parallel", "arbitrary")),
        name="dsa",
    )(proj, proj, wit, kin.reshape(bsz, s, IDX_DIM), ckvn.reshape(bsz, s, KV_RANK), ckvt, wuk, wuvt)


def _mixout_kernel(gr_ref, ga_ref, yr_ref, ya_ref, x_ref, wo_ref, x1_ref):
    mixed = jax.nn.sigmoid(gr_ref[...]) * yr_ref[...] + jax.nn.sigmoid(ga_ref[...]) * ya_ref[...]
    x1_ref[...] = x_ref[...] + _mm(mixed.astype(BF16), wo_ref[...])


def _mixout(proj, y_rnn, y_attn, x2d, wo, tm):
    t, d = x2d.shape
    row = lambda i: (i, 0)
    return pl.pallas_call(
        _mixout_kernel,
        grid=(t // tm,),
        in_specs=[
            pl.BlockSpec((tm, d), lambda i: (i, OFF_GR // D_MODEL)),
            pl.BlockSpec((tm, d), lambda i: (i, OFF_GA // D_MODEL)),
            pl.BlockSpec((tm, d), row),
            pl.BlockSpec((tm, d), row),
            pl.BlockSpec((tm, d), row),
            pl.BlockSpec((d, d), lambda i: (0, 0), pipeline_mode=pl.Buffered(1)),
        ],
        out_specs=pl.BlockSpec((tm, d), row),
        out_shape=jax.ShapeDtypeStruct((t, d), F32),
        compiler_params=_cparams(("parallel",)),
        name="mixout",
    )(proj, proj, y_rnn, y_attn, x2d, wo)


def _peerq_kernel(x1_ref, g_ref, wq_ref, k1_ref, k2_ref, xn_ref, s1_ref, s2_ref):
    x = x1_ref[...]
    xn = (x * lax.rsqrt(jnp.mean(x * x, axis=-1, keepdims=True) + EPS)) * g_ref[...]
    xn_ref[...] = xn
    qp = _mm(xn.astype(BF16), wq_ref[...])
    half = PEER_QDIM // 2
    for h in range(PEER_HEADS):
        qa = qp[:, h * PEER_QDIM:h * PEER_QDIM + half].astype(BF16)
        qb = qp[:, h * PEER_QDIM + half:(h + 1) * PEER_QDIM].astype(BF16)
        s1_ref[h] = _mm_nt(k1_ref[...], qa)
        s2_ref[h] = _mm_nt(k2_ref[...], qb)


def _peerq(x1, g, wq, k1, k2, tm):
    t, d = x1.shape
    sc_spec = pl.BlockSpec((PEER_HEADS, PEER_KEYS, tm), lambda i: (0, 0, i))
    sc_shape = jax.ShapeDtypeStruct((PEER_HEADS, PEER_KEYS, t), F32)
    return pl.pallas_call(
        _peerq_kernel,
        grid=(t // tm,),
        in_specs=[
            pl.BlockSpec((tm, d), lambda i: (i, 0)),
            pl.BlockSpec((1, d), lambda i: (0, 0)),
            pl.BlockSpec((d, PEER_HEADS * PEER_QDIM), lambda i: (0, 0), pipeline_mode=pl.Buffered(1)),
            pl.BlockSpec((PEER_KEYS, PEER_QDIM // 2), lambda i: (0, 0)),
            pl.BlockSpec((PEER_KEYS, PEER_QDIM // 2), lambda i: (0, 0)),
        ],
        out_specs=[pl.BlockSpec((tm, d), lambda i: (i, 0)), sc_spec, sc_spec],
        out_shape=[jax.ShapeDtypeStruct((t, d), F32), sc_shape, sc_shape],
        compiler_params=_cparams(("parallel",)),
        name="peerq",
    )(x1, g, wq, k1, k2)


def _top16_rows(v, pos, payload=None):
    vals, poss, pays = [], [], []
    for _ in range(PEER_TOPK):
        m = jnp.max(v, axis=0, keepdims=True)
        p = jnp.min(jnp.where(v == m, pos, jnp.inf), axis=0, keepdims=True)
        hit = pos == p
        vals.append(m)
        poss.append(p)
        if payload is not None:
            pays.append(jnp.sum(jnp.where(hit, payload, 0), axis=0, keepdims=True))
        v = jnp.where(hit, -jnp.inf, v)
    return vals, poss, pays


def _route_kernel(s1_ref, s2_ref, ids_ref, g_ref):
    tg = s1_ref.shape[2]
    k = PEER_TOPK
    key_pos = lax.broadcasted_iota(jnp.int32, (PEER_KEYS, tg), 0).astype(F32)
    b8 = lax.broadcasted_iota(jnp.int32, (8, tg), 0).astype(F32)
    b16 = lax.broadcasted_iota(jnp.int32, (k, tg), 0).astype(F32)
    cand_pos = jnp.concatenate(
        [b16] + [b8 + float(a * k) for a in range(1, 8)] + [(b8 + 8.0) * float(k)], axis=0)
    for h in range(PEER_HEADS):
        v1, p1, _ = _top16_rows(s1_ref[h], key_pos)
        v2, p2, _ = _top16_rows(s2_ref[h], key_pos)
        v1c = jnp.concatenate(v1, axis=0)
        v2c = jnp.concatenate(v2, axis=0)
        i1c = jnp.concatenate(p1, axis=0).astype(jnp.int32) * PEER_KEYS
        i2c = jnp.concatenate(p2, axis=0).astype(jnp.int32)
        cand_s = jnp.concatenate(
            [v1c[0:1] + v2c] + [v1c[a:a + 1] + v2c[:8] for a in range(1, 8)] + [v1c[8:] + v2c[0:1]],
            axis=0)
        cand_i = jnp.concatenate(
            [i1c[0:1] + i2c] + [i1c[a:a + 1] + i2c[:8] for a in range(1, 8)] + [i1c[8:] + i2c[0:1]],
            axis=0)
        top_s, _, experts = _top16_rows(cand_s, cand_pos, payload=cand_i)
        ts = jnp.concatenate(top_s, axis=0)
        e = jnp.exp(ts - ts[0:1, :])
        gate = e / jnp.sum(e, axis=0, keepdims=True)
        ids_ref[0, h * k:(h + 1) * k, :] = jnp.concatenate(experts, axis=0)
        g_ref[0, h * k:(h + 1) * k, :] = gate


def _route(s1t, s2t, tg):
    t = s1t.shape[2]
    ng = t // tg
    sc_spec = pl.BlockSpec((PEER_HEADS, PEER_KEYS, tg), lambda i: (0, 0, i))
    out_spec = pl.BlockSpec((1, PEER_SLOTS, tg), lambda i: (i, 0, 0))
    return pl.pallas_call(
        _route_kernel,
        grid=(ng,),
        in_specs=[sc_spec, sc_spec],
        out_specs=[out_spec, out_spec],
        out_shape=[jax.ShapeDtypeStruct((ng, PEER_SLOTS, tg), jnp.int32),
                   jax.ShapeDtypeStruct((ng, PEER_SLOTS, tg), F32)],
        compiler_params=_cparams(("parallel",)),
        name="route",
    )(s1t, s2t)


N_GATHER_BUFS = 8
GATHER_AHEAD = N_GATHER_BUFS - 1
U_PHASE_ISSUE = 1
ISSUE_EVERY = 2
LANE = 128
N_CHUNKS = D_MODEL // LANE


class _RowIssuer:
    def __init__(self, start_row):
        self._start_row, self._next = start_row, 0

    def issue(self, n):
        for j in range(self._next, min(self._next + n, PEER_SLOTS)):
            self._start_row(j)
        self._next = min(self._next + n, PEER_SLOTS)


def _experts_kernel(ids_ref, idsn_ref, g_ref, xn_ref, x1_ref, gf_ref, uv_ref, y_ref, buf, sem, acc_ref, *, tg):
    i = pl.program_id(0)
    nb = N_GATHER_BUFS
    lane_t = lax.broadcasted_iota(jnp.int32, (PEER_SLOTS, tg), 1)

    def start_row(src_ids, tok, j, slot):
        e = src_ids[0, tok, j]
        prio = j % 2 if isinstance(j, int) else 0
        pltpu.make_async_copy(uv_ref.at[e], buf.at[slot, pl.ds(j, 1), :], sem.at[slot]).start(priority=prio)

    def wait_token(slot):
        pltpu.make_async_copy(buf.at[slot], buf.at[slot], sem.at[slot]).wait()

    @pl.when(i == 0)
    def _():
        for p in range(GATHER_AHEAD):
            def row(j, carry, p=p):
                start_row(ids_ref, p, j, p)
                return carry
            lax.fori_loop(0, PEER_SLOTS, row, 0)

    def u_phase(tok, slot, issuer, per_chunk):
        xrow = xn_ref[pl.ds(tok, 1), :]
        zpart = jnp.zeros((PEER_SLOTS, LANE), F32)
        for c in range(N_CHUNKS):
            sl = slice(c * LANE, (c + 1) * LANE)
            w = buf[slot, :, sl]
            zpart = zpart + pltpu.bitcast(w & jnp.uint32(0xFFFF0000), F32) * xrow[:, sl]
            if (c + 1) % ISSUE_EVERY == 0:
                issuer.issue(per_chunk * ISSUE_EVERY)
        z = jnp.sum(zpart, axis=-1, keepdims=True)
        gcol = jnp.sum(jnp.where(lane_t == tok, g_ref[0], 0.0), axis=-1, keepdims=True)
        return z, gcol

    def v_phase(tok, slot, act, issuer, per_chunk):
        out = []
        for c in range(N_CHUNKS):
            w = buf[slot, :, c * LANE:(c + 1) * LANE]
            out.append(jnp.sum(act * pltpu.bitcast(w << 16, F32), axis=0, keepdims=True))
            if (c + 1) % ISSUE_EVERY == 0:
                issuer.issue(per_chunk * ISSUE_EVERY)
        acc_ref[pl.ds(tok, 1), :] = jnp.concatenate(out, axis=-1)

    def activation(z, gcol):
        return (0.5 * z * (1.0 + lax.erf(z * (2.0 ** -0.5)))) * gcol

    def step(tok, r, act, src_ids, ntok, has_next):
        issuer = _RowIssuer(lambda j: start_row(src_ids, ntok, j, (r + GATHER_AHEAD) % nb))
        if has_next:
            wait_token((r + 1) % nb)
            z, gcol = u_phase(tok + 1, (r + 1) % nb, issuer, U_PHASE_ISSUE)
            v_phase(tok, r, act, issuer, PEER_SLOTS // N_CHUNKS - U_PHASE_ISSUE)
            return activation(z, gcol)
        v_phase(tok, r, act, issuer, PEER_SLOTS // N_CHUNKS)
        return act

    wait_token(0)
    act0 = activation(*u_phase(0, 0, _RowIssuer(lambda j: None), 0))

    def main_body(q, act):
        for r in range(nb):
            tok = q * nb + r
            act = step(tok, r, act, ids_ref, tok + GATHER_AHEAD, True)
        return act

    act = lax.fori_loop(0, tg // nb - 1, main_body, act0)
    for tok in range(tg - nb, tg):
        ntok = tok + GATHER_AHEAD
        src_ids, ntok = (ids_ref, ntok) if ntok < tg else (idsn_ref, ntok - tg)
        act = step(tok, tok % nb, act, src_ids, ntok, tok + 1 < tg)

    @pl.when(i == pl.num_programs(0) - 1)
    def _():
        for p in range(GATHER_AHEAD):
            wait_token(p)

    xf = x1_ref[...] + acc_ref[...]
    y_ref[...] = (xf * lax.rsqrt(jnp.mean(xf * xf, axis=-1, keepdims=True) + EPS)) * gf_ref[...]


def _experts(ids, gates, xn, x1, gf, uv, tg):
    t, d = x1.shape
    ng = t // tg
    assert tg % N_GATHER_BUFS == 0 and tg > GATHER_AHEAD
    ids_tok = jnp.transpose(ids, (0, 2, 1))
    kern = functools.partial(_experts_kernel, tg=tg)
    return pl.pallas_call(
        kern,
        grid=(ng,),
        in_specs=[
            pl.BlockSpec((1, tg, PEER_SLOTS), lambda i: (i, 0, 0), memory_space=pltpu.SMEM),
            pl.BlockSpec((1, tg, PEER_SLOTS), lambda i: (jnp.minimum(i + 1, ng - 1), 0, 0),
                         memory_space=pltpu.SMEM),
            pl.BlockSpec((1, PEER_SLOTS, tg), lambda i: (i, 0, 0)),
            pl.BlockSpec((tg, d), lambda i: (i, 0)),
            pl.BlockSpec((tg, d), lambda i: (i, 0)),
            pl.BlockSpec((1, d), lambda i: (0, 0)),
            pl.BlockSpec(memory_space=pl.ANY),
        ],
        out_specs=pl.BlockSpec((tg, d), lambda i: (i, 0)),
        out_shape=jax.ShapeDtypeStruct((t, d), F32),
        scratch_shapes=[
            pltpu.VMEM((N_GATHER_BUFS, PEER_SLOTS, d), jnp.uint32),
            pltpu.SemaphoreType.DMA((N_GATHER_BUFS,)),
            pltpu.VMEM((tg, d), F32),
        ],
        compiler_params=_cparams(("arbitrary",)),
        name="experts",
    )(ids_tok, ids_tok, gates, xn, x1, gf, uv)


def _regroup_w_in(w):
    d = w.shape[0]
    z = lambda n: jnp.zeros((d, n), w.dtype)
    xr_q = w[:, 0:4096]
    ckv = w[:, 4096:4608]
    qi = w[:, 4608:5632]
    ki = w[:, 5632:5696]
    wi = w[:, 5696:5712]
    gr_ga = w[:, 5712:9808]
    return jnp.concatenate([xr_q, gr_ga, qi, ckv, ki, z(64), wi, z(112)], axis=1).astype(BF16)


def _pack_uv(u, v):
    ub = lax.bitcast_convert_type(u.astype(BF16), jnp.uint16).astype(jnp.uint32)
    vb = lax.bitcast_convert_type(v.astype(BF16), jnp.uint16).astype(jnp.uint32)
    return ((ub << 16) | vb)[:, None, :]


def _layer(x, norm_mix_g, w_in, conv_w, conv_b, rg_wa, rg_ba, rg_wx, rg_bx, rg_lambda,
           kv_norm_g, w_uk, w_uv, idx_ln_g, idx_ln_b, w_o, norm_ffn_g, peer_wq,
           peer_keys1, peer_keys2, peer_u, peer_v, norm_final_g, *,
           tm_in, tn_in, tc_rnn, tp, ks, tm_mix, tm_pq, tg):
    bsz, s, d = x.shape
    t = bsz * s
    x2d = x.reshape(t, d)
    row = lambda a: a.reshape(1, -1)

    proj = _inproj(x2d, row(norm_mix_g), _regroup_w_in(w_in), tm_in, tn_in)
    y_rnn = _rglru(proj, conv_w, row(conv_b), rg_wa.astype(BF16), row(rg_ba),
                   rg_wx.astype(BF16), row(rg_bx), row(rg_lambda), bsz, s, tc_rnn)
    ckvn, ckvt, kin, wit = _prep(proj, row(kv_norm_g), row(idx_ln_g), row(idx_ln_b), bsz, s, tp)
    y_attn = _dsa(proj, wit, kin, ckvn, ckvt, w_uk.astype(BF16),
                  jnp.transpose(w_uv, (0, 2, 1)).astype(BF16), bsz, s, ks)
    x1 = _mixout(proj, y_rnn, y_attn, x2d, w_o.astype(BF16), tm_mix)
    xn, s1t, s2t = _peerq(x1, row(norm_ffn_g), peer_wq.astype(BF16),
                          peer_keys1.astype(BF16), peer_keys2.astype(BF16), tm_pq)
    ids, gates = _route(s1t, s2t, tg)
    y = _experts(ids, gates, xn, x1, row(norm_final_g), _pack_uv(peer_u, peer_v), tg)
    return y.reshape(bsz, s, d)


def kernel(x, norm_mix_g, w_in, conv_w, conv_b, rg_wa, rg_ba, rg_wx, rg_bx, rg_lambda,
           kv_norm_g, w_uk, w_uv, idx_ln_g, idx_ln_b, w_o, norm_ffn_g, peer_wq,
           peer_keys1, peer_keys2, peer_u, peer_v, norm_final_g):
    assert norm_mix_g.shape[0] == 1, "single-layer trunk"
    s = x.shape[1]
    return _layer(
        x, norm_mix_g[0], w_in[0], conv_w[0], conv_b[0], rg_wa[0], rg_ba[0], rg_wx[0], rg_bx[0],
        rg_lambda[0], kv_norm_g[0], w_uk[0], w_uv[0], idx_ln_g[0], idx_ln_b[0], w_o[0],
        norm_ffn_g[0], peer_wq[0], peer_keys1[0], peer_keys2[0], peer_u[0], peer_v[0], norm_final_g,
        tm_in=min(1024, s), tn_in=1664, tc_rnn=min(512, s), tp=min(512, s), ks=min(512, s),
        tm_mix=min(256, s), tm_pq=min(256, s), tg=128)
```

```python
import functools

import jax
import jax.numpy as jnp
import numpy as np
from jax import lax
from jax.experimental import pallas as pl
from jax.experimental.pallas import tpu as pltpu

D_MODEL = 2048
RNN_BLOCKS = 16
RNN_BW = D_MODEL // RNN_BLOCKS
CONV_WIDTH = 4
RG_C = 8.0
N_HEADS = 16
HEAD_DIM = 128
KV_RANK = 512
IDX_HEADS = 16
IDX_DIM = 64
TOPK_MAX = 256
Q_BLOCK = 128
PEER_HEADS = 8
PEER_KEYS = 128
PEER_QDIM = 256
PEER_TOPK = 16
PEER_SLOTS = PEER_HEADS * PEER_TOPK
EPS = 1e-6

OFF_XR, OFF_Q, OFF_GR, OFF_GA = 0, 2048, 4096, 6144
OFF_QI, OFF_CKV, OFF_KI, OFF_WI = 8192, 9216, 9728, 9856
N_PROJ = 9984

V7X_VMEM_LIMIT = 56 * 1024 * 1024
COUNT_ROWS = 64
LOGIT_SCALE_LOG2 = HEAD_DIM ** -0.5 * float(np.log2(np.e))
INT_MIN = -(2 ** 31)
NEG_BIG = -1e30

BF16 = jnp.bfloat16
F32 = jnp.float32
NT_DIMS = (((1,), (1,)), ((), ()))


def _mm(a, b):
    return jnp.dot(a, b, preferred_element_type=F32)


def _mm_nt(a, b):
    return lax.dot_general(a, b, NT_DIMS, preferred_element_type=F32)


def _cparams(sem):
    return pltpu.CompilerParams(dimension_semantics=sem, vmem_limit_bytes=V7X_VMEM_LIMIT)


def _inproj_kernel(x_ref, g_ref, w_ref, o_ref, xn_ref):
    @pl.when(pl.program_id(1) == 0)
    def _():
        x = x_ref[...]
        ms = jnp.mean(x * x, axis=-1, keepdims=True)
        xn_ref[...] = ((x * lax.rsqrt(ms + EPS)) * g_ref[...]).astype(BF16)

    o_ref[...] = _mm(xn_ref[...], w_ref[...])


def _inproj(x2d, g, w_r, tm, tn):
    t, d = x2d.shape
    n = w_r.shape[1]
    return pl.pallas_call(
        _inproj_kernel,
        grid=(t // tm, n // tn),
        in_specs=[
            pl.BlockSpec((tm, d), lambda i, j: (i, 0)),
            pl.BlockSpec((1, d), lambda i, j: (0, 0)),
            pl.BlockSpec((d, tn), lambda i, j: (0, j)),
        ],
        out_specs=pl.BlockSpec((tm, tn), lambda i, j: (i, j)),
        out_shape=jax.ShapeDtypeStruct((t, n), F32),
        scratch_shapes=[pltpu.VMEM((tm, d), BF16)],
        compiler_params=_cparams(("parallel", "arbitrary")),
        name="inproj",
    )(x2d, g, w_r)


def _rglru_kernel(x_ref, cw_ref, cb_ref, wa_ref, ba_ref, wx_ref, bx_ref, lam_ref,
                  o_ref, xprev_ref, hprev_ref):
    @pl.when(pl.program_id(2) == 0)
    def _():
        xprev_ref[...] = jnp.zeros_like(xprev_ref)
        hprev_ref[...] = jnp.zeros_like(hprev_ref)

    x = x_ref[...]
    tc = x.shape[0]
    prev8 = xprev_ref[...]
    rows8 = lax.broadcasted_iota(jnp.int32, (8, RNN_BW), 0)
    y = cb_ref[...] + cw_ref[CONV_WIDTH - 1:CONV_WIDTH, :] * x
    for k in range(1, CONV_WIDTH):
        r = pltpu.roll(x, k, 0)
        pr = pltpu.roll(prev8, k, 0)
        top = jnp.where(rows8 < k, pr, r[:8])
        xs = jnp.concatenate([top, r[8:]], axis=0)
        y = y + cw_ref[CONV_WIDTH - 1 - k:CONV_WIDTH - k, :] * xs
    xprev_ref[...] = x[tc - 8:, :]

    xb = y.astype(BF16)
    r_g = jax.nn.sigmoid(_mm(xb, wa_ref[0]) + ba_ref[...])
    i_g = jax.nn.sigmoid(_mm(xb, wx_ref[0]) + bx_ref[...])
    nl = -lam_ref[...]
    sp = jnp.maximum(nl, 0.0) + jnp.log1p(jnp.exp(-jnp.abs(nl)))
    log_a = (-RG_C) * r_g * sp
    a = jnp.exp(log_a)
    u = jnp.sqrt(-jnp.tanh(log_a) * (a * a + 1.0)) * (i_g * y)

    rows = lax.broadcasted_iota(jnp.int32, (tc, RNN_BW), 0)
    d = 1
    while d < tc:
        a_sh = pltpu.roll(a, d, 0)
        u_sh = pltpu.roll(u, d, 0)
        m = rows >= d
        u = jnp.where(m, a * u_sh + u, u)
        a = jnp.where(m, a * a_sh, a)
        d *= 2
    h = u + a * hprev_ref[...]
    o_ref[...] = h
    hprev_ref[...] = h[tc - 1:tc, :]


def _rglru(proj, conv_w, conv_b, wa, ba, wx, bx, lam, bsz, s, tc):
    t = bsz * s
    nc = s // tc
    vec = lambda: pl.BlockSpec((1, RNN_BW), lambda b, n, c: (0, n))
    return pl.pallas_call(
        _rglru_kernel,
        grid=(bsz, RNN_BLOCKS, nc),
        in_specs=[
            pl.BlockSpec((tc, RNN_BW), lambda b, n, c: (b * nc + c, n)),
            pl.BlockSpec((CONV_WIDTH, RNN_BW), lambda b, n, c: (0, n)),
            vec(),
            pl.BlockSpec((1, RNN_BW, RNN_BW), lambda b, n, c: (n, 0, 0)),
            vec(),
            pl.BlockSpec((1, RNN_BW, RNN_BW), lambda b, n, c: (n, 0, 0)),
            vec(),
            vec(),
        ],
        out_specs=pl.BlockSpec((tc, RNN_BW), lambda b, n, c: (b * nc + c, n)),
        out_shape=jax.ShapeDtypeStruct((t, D_MODEL), F32),
        scratch_shapes=[pltpu.VMEM((8, RNN_BW), F32), pltpu.VMEM((1, RNN_BW), F32)],
        compiler_params=_cparams(("parallel", "parallel", "arbitrary")),
        name="rglru",
    )(proj, conv_w, conv_b, wa, ba, wx, bx, lam)


def _prep_kernel(ckv_ref, ki_ref, wi_ref, kvg_ref, lng_ref, lnb_ref,
                 ckvn_ref, ckvt_ref, kin_ref, wit_ref):
    c = ckv_ref[...]
    cn = (c * lax.rsqrt(jnp.mean(c * c, axis=-1, keepdims=True) + EPS)) * kvg_ref[...]
    ckvn_ref[...] = cn.astype(BF16)
    ckvt_ref[0] = cn.T.astype(BF16)
    k = ki_ref[...][:, :IDX_DIM]
    mu = jnp.mean(k, axis=-1, keepdims=True)
    var = jnp.mean(jnp.square(k - mu), axis=-1, keepdims=True)
    kn = (k - mu) * lax.rsqrt(var + EPS)
    kin_ref[...] = (kn * lng_ref[...] + lnb_ref[...]).astype(BF16)
    w = wi_ref[...] * (IDX_HEADS ** -0.5 * IDX_DIM ** -0.5)
    wit_ref[0] = w.T[:IDX_HEADS, :]


def _prep(proj, kvg, lng, lnb, bsz, s, tp):
    t = bsz * s
    nc = s // tp
    return pl.pallas_call(
        _prep_kernel,
        grid=(bsz, nc),
        in_specs=[
            pl.BlockSpec((tp, KV_RANK), lambda b, c: (b * nc + c, OFF_CKV // KV_RANK)),
            pl.BlockSpec((tp, 128), lambda b, c: (b * nc + c, OFF_KI // 128)),
            pl.BlockSpec((tp, 128), lambda b, c: (b * nc + c, OFF_WI // 128)),
            pl.BlockSpec((1, KV_RANK), lambda b, c: (0, 0)),
            pl.BlockSpec((1, IDX_DIM), lambda b, c: (0, 0)),
            pl.BlockSpec((1, IDX_DIM), lambda b, c: (0, 0)),
        ],
        out_specs=[
            pl.BlockSpec((tp, KV_RANK), lambda b, c: (b * nc + c, 0)),
            pl.BlockSpec((1, KV_RANK, tp), lambda b, c: (b, 0, c)),
            pl.BlockSpec((tp, IDX_DIM), lambda b, c: (b * nc + c, 0)),
            pl.BlockSpec((1, IDX_HEADS, tp), lambda b, c: (b, 0, c)),
        ],
        out_shape=[
            jax.ShapeDtypeStruct((t, KV_RANK), BF16),
            jax.ShapeDtypeStruct((bsz, KV_RANK, s), BF16),
            jax.ShapeDtypeStruct((t, IDX_DIM), BF16),
            jax.ShapeDtypeStruct((bsz, IDX_HEADS, s), F32),
        ],
        compiler_params=_cparams(("parallel", "parallel")),
        name="prep",
    )(proj, proj, proj, kvg, lng, lnb)


def _dsa_kernel(q_ref, qi_ref, wit_ref, k_ref, ckv_ref, ckvt_ref, wuk_ref, wuvt_ref, o_ref,
                keys_sc, qit_sc, qlat_sc, acc_sc, m_sc, l_sc, alpha_sc, p_sc, *, ks, topk):
    qb = pl.program_id(1)
    t0 = qb * Q_BLOCK
    nkb = (t0 + Q_BLOCK + ks - 1) // ks
    hd = HEAD_DIM

    for h in range(N_HEADS):
        qh = q_ref[:, h * hd:(h + 1) * hd].astype(BF16)
        qlat_sc[:, h * Q_BLOCK:(h + 1) * Q_BLOCK] = (_mm_nt(wuk_ref[h], qh) * LOGIT_SCALE_LOG2).astype(BF16)
    eye = jnp.where(lax.broadcasted_iota(jnp.int32, (IDX_DIM, IDX_DIM), 0)
                    == lax.broadcasted_iota(jnp.int32, (IDX_DIM, IDX_DIM), 1), 1.0, 0.0).astype(BF16)
    for h in range(IDX_HEADS):
        qih = qi_ref[:, h * IDX_DIM:(h + 1) * IDX_DIM].astype(BF16)
        qit_sc[:, h * Q_BLOCK:(h + 1) * Q_BLOCK] = _mm_nt(eye, qih).astype(BF16)

    lane_q = t0 + lax.broadcasted_iota(jnp.int32, (ks, Q_BLOCK), 1)
    row_i = lax.broadcasted_iota(jnp.int32, (ks, Q_BLOCK), 0)

    def score_body(kb, carry):
        s0 = pl.multiple_of(kb * ks, ks)
        kblk = k_ref[0, pl.ds(s0, ks), :]
        r_all = _mm(kblk, qit_sc[...])
        acc = jnp.zeros((ks, Q_BLOCK), F32)
        for h in range(IDX_HEADS):
            r = r_all[:, h * Q_BLOCK:(h + 1) * Q_BLOCK]
            acc = acc + jnp.maximum(r, 0.0) * wit_ref[0, h:h + 1, :]
        bits = pltpu.bitcast(acc, jnp.int32)
        key = bits ^ ((bits >> 31) & jnp.int32(0x7FFFFFFF))
        key = jnp.where(s0 + row_i <= lane_q, key, jnp.int32(INT_MIN))
        keys_sc[pl.ds(s0, ks), :] = key
        return carry

    lax.fori_loop(0, nkb, score_body, 0)

    def count_ge(cand):
        def body(kb, c):
            s0 = pl.multiple_of(kb * ks, ks)
            blk = keys_sc[pl.ds(s0, ks), :]
            hit = jnp.where(blk >= cand, 1.0, 0.0)
            return c + jnp.sum(hit.reshape(ks // COUNT_ROWS, COUNT_ROWS, Q_BLOCK), axis=0)
        cpart = lax.fori_loop(0, nkb, body, jnp.zeros((COUNT_ROWS, Q_BLOCK), F32))
        return jnp.sum(cpart, axis=0, keepdims=True)

    def bit_body(i, tu):
        bit = lax.shift_left(jnp.int32(1), 31 - i)
        cand_u = tu | bit
        cnt = count_ge(cand_u ^ jnp.int32(INT_MIN))
        return jnp.where(cnt >= float(topk), cand_u, tu)

    tu = lax.fori_loop(0, 32, bit_body, jnp.zeros((1, Q_BLOCK), jnp.int32))
    tsel = jnp.maximum(tu ^ jnp.int32(INT_MIN), jnp.int32(INT_MIN + 1))

    m_sc[...] = jnp.full_like(m_sc, NEG_BIG)
    l_sc[...] = jnp.zeros_like(l_sc)
    acc_sc[...] = jnp.zeros_like(acc_sc)

    def attn_body(kb, carry):
        s0 = pl.multiple_of(kb * ks, ks)
        cblk = ckv_ref[0, pl.ds(s0, ks), :]
        logt = _mm(cblk, qlat_sc[...])
        sel = keys_sc[pl.ds(s0, ks), :] >= tsel
        for h in range(N_HEADS):
            sl = slice(h * Q_BLOCK, (h + 1) * Q_BLOCK)
            lg = jnp.where(sel, logt[:, sl], NEG_BIG)
            mb = jnp.max(lg, axis=0, keepdims=True)
            mo = m_sc[:, sl]
            mn = jnp.maximum(mo, mb)
            p = jnp.exp2(lg - mn)
            alpha = jnp.exp2(mo - mn)
            l_sc[:, sl] = alpha * l_sc[:, sl] + jnp.sum(p, axis=0, keepdims=True)
            m_sc[:, sl] = mn
            alpha_sc[:, sl] = alpha
            p_sc[:, sl] = p.astype(BF16)
        ctb = ckvt_ref[0, :, pl.ds(s0, ks)]
        acc_sc[...] = acc_sc[...] * alpha_sc[...] + _mm(ctb, p_sc[...])
        return carry

    lax.fori_loop(0, nkb, attn_body, 0)

    inv_l = 1.0 / l_sc[...]
    for h in range(N_HEADS):
        sl = slice(h * Q_BLOCK, (h + 1) * Q_BLOCK)
        ol = (acc_sc[:, sl] * inv_l[:, sl]).astype(BF16)
        ot = _mm(wuvt_ref[h], ol)
        o_ref[:, h * hd:(h + 1) * hd] = ot.T


def _dsa(proj, wit, kin, ckvn, ckvt, wuk, wuvt, bsz, s, ks):
    t = bsz * s
    nq = s // Q_BLOCK
    topk = min(TOPK_MAX, s // 4)
    one = pl.Buffered(1)
    kern = functools.partial(_dsa_kernel, ks=ks, topk=topk)
    return pl.pallas_call(
        kern,
        grid=(bsz, nq),
        in_specs=[
            pl.BlockSpec((Q_BLOCK, D_MODEL), lambda b, i: (b * nq + i, OFF_Q // D_MODEL)),
            pl.BlockSpec((Q_BLOCK, IDX_HEADS * IDX_DIM), lambda b, i: (b * nq + i, OFF_QI // 1024)),
            pl.BlockSpec((1, IDX_HEADS, Q_BLOCK), lambda b, i: (b, 0, i)),
            pl.BlockSpec((1, s, IDX_DIM), lambda b, i: (b, 0, 0), pipeline_mode=one),
            pl.BlockSpec((1, s, KV_RANK), lambda b, i: (b, 0, 0), pipeline_mode=one),
            pl.BlockSpec((1, KV_RANK, s), lambda b, i: (b, 0, 0), pipeline_mode=one),
            pl.BlockSpec((N_HEADS, KV_RANK, HEAD_DIM), lambda b, i: (0, 0, 0), pipeline_mode=one),
            pl.BlockSpec((N_HEADS, HEAD_DIM, KV_RANK), lambda b, i: (0, 0, 0), pipeline_mode=one),
        ],
        out_specs=pl.BlockSpec((Q_BLOCK, D_MODEL), lambda b, i: (b * nq + i, 0)),
        out_shape=jax.ShapeDtypeStruct((t, D_MODEL), F32),
        scratch_shapes=[
            pltpu.VMEM((s, Q_BLOCK), jnp.int32),
            pltpu.VMEM((IDX_DIM, IDX_HEADS * Q_BLOCK), BF16),
            pltpu.VMEM((KV_RANK, N_HEADS * Q_BLOCK), BF16),
            pltpu.VMEM((KV_RANK, N_HEADS * Q_BLOCK), F32),
            pltpu.VMEM((1, N_HEADS * Q_BLOCK), F32),
            pltpu.VMEM((1, N_HEADS * Q_BLOCK), F32),
            pltpu.VMEM((1, N_HEADS * Q_BLOCK), F32),
            pltpu.VMEM((ks, N_HEADS * Q_BLOCK), BF16),
        ],
        compiler_params=_cparams(("parallel", "arbitrary")),
        name="dsa",
    )(proj, proj, wit, kin.reshape(bsz, s, IDX_DIM), ckvn.reshape(bsz, s, KV_RANK), ckvt, wuk, wuvt)


def _mixout_kernel(gr_ref, ga_ref, yr_ref, ya_ref, x_ref, wo_ref, x1_ref):
    mixed = jax.nn.sigmoid(gr_ref[...]) * yr_ref[...] + jax.nn.sigmoid(ga_ref[...]) * ya_ref[...]
    x1_ref[...] = x_ref[...] + _mm(mixed.astype(BF16), wo_ref[...])


def _mixout(proj, y_rnn, y_attn, x2d, wo, tm):
    t, d = x2d.shape
    row = lambda i: (i, 0)
    return pl.pallas_call(
        _mixout_kernel,
        grid=(t // tm,),
        in_specs=[
            pl.BlockSpec((tm, d), lambda i: (i, OFF_GR // D_MODEL)),
            pl.BlockSpec((tm, d), lambda i: (i, OFF_GA // D_MODEL)),
            pl.BlockSpec((tm, d), row),
            pl.BlockSpec((tm, d), row),
            pl.BlockSpec((tm, d), row),
            pl.BlockSpec((d, d), lambda i: (0, 0), pipeline_mode=pl.Buffered(1)),
        ],
        out_specs=pl.BlockSpec((tm, d), row),
        out_shape=jax.ShapeDtypeStruct((t, d), F32),
        compiler_params=_cparams(("parallel",)),
        name="mixout",
    )(proj, proj, y_rnn, y_attn, x2d, wo)


def _peerq_kernel(x1_ref, g_ref, wq_ref, k1_ref, k2_ref, xn_ref, s1_ref, s2_ref):
    x = x1_ref[...]
    xn = (x * lax.rsqrt(jnp.mean(x * x, axis=-1, keepdims=True) + EPS)) * g_ref[...]
    xn_ref[...] = xn
    qp = _mm(xn.astype(BF16), wq_ref[...])
    half = PEER_QDIM // 2
    for h in range(PEER_HEADS):
        qa = qp[:, h * PEER_QDIM:h * PEER_QDIM + half].astype(BF16)
        qb = qp[:, h * PEER_QDIM + half:(h + 1) * PEER_QDIM].astype(BF16)
        s1_ref[h] = _mm_nt(k1_ref[...], qa)
        s2_ref[h] = _mm_nt(k2_ref[...], qb)


def _peerq(x1, g, wq, k1, k2, tm):
    t, d = x1.shape
    sc_spec = pl.BlockSpec((PEER_HEADS, PEER_KEYS, tm), lambda i: (0, 0, i))
    sc_shape = jax.ShapeDtypeStruct((PEER_HEADS, PEER_KEYS, t), F32)
    return pl.pallas_call(
        _peerq_kernel,
        grid=(t // tm,),
        in_specs=[
            pl.BlockSpec((tm, d), lambda i: (i, 0)),
            pl.BlockSpec((1, d), lambda i: (0, 0)),
            pl.BlockSpec((d, PEER_HEADS * PEER_QDIM), lambda i: (0, 0), pipeline_mode=pl.Buffered(1)),
            pl.BlockSpec((PEER_KEYS, PEER_QDIM // 2), lambda i: (0, 0)),
            pl.BlockSpec((PEER_KEYS, PEER_QDIM // 2), lambda i: (0, 0)),
        ],
        out_specs=[pl.BlockSpec((tm, d), lambda i: (i, 0)), sc_spec, sc_spec],
        out_shape=[jax.ShapeDtypeStruct((t, d), F32), sc_shape, sc_shape],
        compiler_params=_cparams(("parallel",)),
        name="peerq",
    )(x1, g, wq, k1, k2)


def _top16_rows(v, pos, payload=None):
    vals, poss, pays = [], [], []
    for _ in range(PEER_TOPK):
        m = jnp.max(v, axis=0, keepdims=True)
        p = jnp.min(jnp.where(v == m, pos, jnp.inf), axis=0, keepdims=True)
        hit = pos == p
        vals.append(m)
        poss.append(p)
        if payload is not None:
            pays.append(jnp.sum(jnp.where(hit, payload, 0), axis=0, keepdims=True))
        v = jnp.where(hit, -jnp.inf, v)
    return vals, poss, pays


def _route_kernel(s1_ref, s2_ref, ids_ref, g_ref):
    tg = s1_ref.shape[2]
    k = PEER_TOPK
    key_pos = lax.broadcasted_iota(jnp.int32, (PEER_KEYS, tg), 0).astype(F32)
    b8 = lax.broadcasted_iota(jnp.int32, (8, tg), 0).astype(F32)
    b16 = lax.broadcasted_iota(jnp.int32, (k, tg), 0).astype(F32)
    cand_pos = jnp.concatenate(
        [b16] + [b8 + float(a * k) for a in range(1, 8)] + [(b8 + 8.0) * float(k)], axis=0)
    for h in range(PEER_HEADS):
        v1, p1, _ = _top16_rows(s1_ref[h], key_pos)
        v2, p2, _ = _top16_rows(s2_ref[h], key_pos)
        v1c = jnp.concatenate(v1, axis=0)
        v2c = jnp.concatenate(v2, axis=0)
        i1c = jnp.concatenate(p1, axis=0).astype(jnp.int32) * PEER_KEYS
        i2c = jnp.concatenate(p2, axis=0).astype(jnp.int32)
        cand_s = jnp.concatenate(
            [v1c[0:1] + v2c] + [v1c[a:a + 1] + v2c[:8] for a in range(1, 8)] + [v1c[8:] + v2c[0:1]],
            axis=0)
        cand_i = jnp.concatenate(
            [i1c[0:1] + i2c] + [i1c[a:a + 1] + i2c[:8] for a in range(1, 8)] + [i1c[8:] + i2c[0:1]],
            axis=0)
        top_s, _, experts = _top16_rows(cand_s, cand_pos, payload=cand_i)
        ts = jnp.concatenate(top_s, axis=0)
        e = jnp.exp(ts - ts[0:1, :])
        gate = e / jnp.sum(e, axis=0, keepdims=True)
        ids_ref[0, h * k:(h + 1) * k, :] = jnp.concatenate(experts, axis=0)
        g_ref[0, h * k:(h + 1) * k, :] = gate


def _route(s1t, s2t, tg):
    t = s1t.shape[2]
    ng = t // tg
    sc_spec = pl.BlockSpec((PEER_HEADS, PEER_KEYS, tg), lambda i: (0, 0, i))
    out_spec = pl.BlockSpec((1, PEER_SLOTS, tg), lambda i: (i, 0, 0))
    return pl.pallas_call(
        _route_kernel,
        grid=(ng,),
        in_specs=[sc_spec, sc_spec],
        out_specs=[out_spec, out_spec],
        out_shape=[jax.ShapeDtypeStruct((ng, PEER_SLOTS, tg), jnp.int32),
                   jax.ShapeDtypeStruct((ng, PEER_SLOTS, tg), F32)],
        compiler_params=_cparams(("parallel",)),
        name="route",
    )(s1t, s2t)


N_GATHER_BUFS = 8
GATHER_AHEAD = N_GATHER_BUFS - 1
U_PHASE_ISSUE = 1
ISSUE_EVERY = 2
LANE = 128
N_CHUNKS = D_MODEL // LANE


class _RowIssuer:
    def __init__(self, start_row):
        self._start_row, self._next = start_row, 0

    def issue(self, n):
        for j in range(self._next, min(self._next + n, PEER_SLOTS)):
            self._start_row(j)
        self._next = min(self._next + n, PEER_SLOTS)


def _experts_kernel(ids_ref, idsn_ref, g_ref, xn_ref, x1_ref, gf_ref, uv_ref, y_ref, buf, sem, acc_ref, *, tg):
    i = pl.program_id(0)
    nb = N_GATHER_BUFS
    lane_t = lax.broadcasted_iota(jnp.int32, (PEER_SLOTS, tg), 1)

    def start_row(src_ids, tok, j, slot):
        e = src_ids[0, tok, j]
        prio = j % 2 if isinstance(j, int) else 0
        pltpu.make_async_copy(uv_ref.at[e], buf.at[slot, pl.ds(j, 1), :], sem.at[slot]).start(priority=prio)

    def wait_token(slot):
        pltpu.make_async_copy(buf.at[slot], buf.at[slot], sem.at[slot]).wait()

    @pl.when(i == 0)
    def _():
        for p in range(GATHER_AHEAD):
            def row(j, carry, p=p):
                start_row(ids_ref, p, j, p)
                return carry
            lax.fori_loop(0, PEER_SLOTS, row, 0)

    def u_phase(tok, slot, issuer, per_chunk):
        xrow = xn_ref[pl.ds(tok, 1), :]
        zpart = jnp.zeros((PEER_SLOTS, LANE), F32)
        for c in range(N_CHUNKS):
            sl = slice(c * LANE, (c + 1) * LANE)
            w = buf[slot, :, sl]
            zpart = zpart + pltpu.bitcast(w & jnp.uint32(0xFFFF0000), F32) * xrow[:, sl]
            if (c + 1) % ISSUE_EVERY == 0:
                issuer.issue(per_chunk * ISSUE_EVERY)
        z = jnp.sum(zpart, axis=-1, keepdims=True)
        gcol = jnp.sum(jnp.where(lane_t == tok, g_ref[0], 0.0), axis=-1, keepdims=True)
        return z, gcol

    def v_phase(tok, slot, act, issuer, per_chunk):
        out = []
        for c in range(N_CHUNKS):
            w = buf[slot, :, c * LANE:(c + 1) * LANE]
            out.append(jnp.sum(act * pltpu.bitcast(w << 16, F32), axis=0, keepdims=True))
            if (c + 1) % ISSUE_EVERY == 0:
                issuer.issue(per_chunk * ISSUE_EVERY)
        acc_ref[pl.ds(tok, 1), :] = jnp.concatenate(out, axis=-1)

    def activation(z, gcol):
        return (0.5 * z * (1.0 + lax.erf(z * (2.0 ** -0.5)))) * gcol

    def step(tok, r, act, src_ids, ntok, has_next):
        issuer = _RowIssuer(lambda j: start_row(src_ids, ntok, j, (r + GATHER_AHEAD) % nb))
        if has_next:
            wait_token((r + 1) % nb)
            z, gcol = u_phase(tok + 1, (r + 1) % nb, issuer, U_PHASE_ISSUE)
            v_phase(tok, r, act, issuer, PEER_SLOTS // N_CHUNKS - U_PHASE_ISSUE)
            return activation(z, gcol)
        v_phase(tok, r, act, issuer, PEER_SLOTS // N_CHUNKS)
        return act

    wait_token(0)
    act0 = activation(*u_phase(0, 0, _RowIssuer(lambda j: None), 0))

    def main_body(q, act):
        for r in range(nb):
            tok = q * nb + r
            act = step(tok, r, act, ids_ref, tok + GATHER_AHEAD, True)
        return act

    act = lax.fori_loop(0, tg // nb - 1, main_body, act0)
    for tok in range(tg - nb, tg):
        ntok = tok + GATHER_AHEAD
        src_ids, ntok = (ids_ref, ntok) if ntok < tg else (idsn_ref, ntok - tg)
        act = step(tok, tok % nb, act, src_ids, ntok, tok + 1 < tg)

    @pl.when(i == pl.num_programs(0) - 1)
    def _():
        for p in range(GATHER_AHEAD):
            wait_token(p)

    xf = x1_ref[...] + acc_ref[...]
    y_ref[...] = (xf * lax.rsqrt(jnp.mean(xf * xf, axis=-1, keepdims=True) + EPS)) * gf_ref[...]


def _experts(ids, gates, xn, x1, gf, uv, tg):
    t, d = x1.shape
    ng = t // tg
    assert tg % N_GATHER_BUFS == 0 and tg > GATHER_AHEAD
    ids_tok = jnp.transpose(ids, (0, 2, 1))
    kern = functools.partial(_experts_kernel, tg=tg)
    return pl.pallas_call(
        kern,
        grid=(ng,),
        in_specs=[
            pl.BlockSpec((1, tg, PEER_SLOTS), lambda i: (i, 0, 0), memory_space=pltpu.SMEM),
            pl.BlockSpec((1, tg, PEER_SLOTS), lambda i: (jnp.minimum(i + 1, ng - 1), 0, 0),
                         memory_space=pltpu.SMEM),
            pl.BlockSpec((1, PEER_SLOTS, tg), lambda i: (i, 0, 0)),
            pl.BlockSpec((tg, d), lambda i: (i, 0)),
            pl.BlockSpec((tg, d), lambda i: (i, 0)),
            pl.BlockSpec((1, d), lambda i: (0, 0)),
            pl.BlockSpec(memory_space=pl.ANY),
        ],
        out_specs=pl.BlockSpec((tg, d), lambda i: (i, 0)),
        out_shape=jax.ShapeDtypeStruct((t, d), F32),
        scratch_shapes=[
            pltpu.VMEM((N_GATHER_BUFS, PEER_SLOTS, d), jnp.uint32),
            pltpu.SemaphoreType.DMA((N_GATHER_BUFS,)),
            pltpu.VMEM((tg, d), F32),
        ],
        compiler_params=_cparams(("arbitrary",)),
        name="experts",
    )(ids_tok, ids_tok, gates, xn, x1, gf, uv)


def _regroup_w_in(w):
    d = w.shape[0]
    z = lambda n: jnp.zeros((d, n), w.dtype)
    xr_q = w[:, 0:4096]
    ckv = w[:, 4096:4608]
    qi = w[:, 4608:5632]
    ki = w[:, 5632:5696]
    wi = w[:, 5696:5712]
    gr_ga = w[:, 5712:9808]
    return jnp.concatenate([xr_q, gr_ga, qi, ckv, ki, z(64), wi, z(112)], axis=1).astype(BF16)


def _pack_uv(u, v):
    ub = lax.bitcast_convert_type(u.astype(BF16), jnp.uint16).astype(jnp.uint32)
    vb = lax.bitcast_convert_type(v.astype(BF16), jnp.uint16).astype(jnp.uint32)
    return ((ub << 16) | vb)[:, None, :]


def _layer(x, norm_mix_g, w_in, conv_w, conv_b, rg_wa, rg_ba, rg_wx, rg_bx, rg_lambda,
           kv_norm_g, w_uk, w_uv, idx_ln_g, idx_ln_b, w_o, norm_ffn_g, peer_wq,
           peer_keys1, peer_keys2, peer_u, peer_v, norm_final_g, *,
           tm_in, tn_in, tc_rnn, tp, ks, tm_mix, tm_pq, tg):
    bsz, s, d = x.shape
    t = bsz * s
    x2d = x.reshape(t, d)
    row = lambda a: a.reshape(1, -1)

    proj = _inproj(x2d, row(norm_mix_g), _regroup_w_in(w_in), tm_in, tn_in)
    y_rnn = _rglru(proj, conv_w, row(conv_b), rg_wa.astype(BF16), row(rg_ba),
                   rg_wx.astype(BF16), row(rg_bx), row(rg_lambda), bsz, s, tc_rnn)
    ckvn, ckvt, kin, wit = _prep(proj, row(kv_norm_g), row(idx_ln_g), row(idx_ln_b), bsz, s, tp)
    y_attn = _dsa(proj, wit, kin, ckvn, ckvt, w_uk.astype(BF16),
                  jnp.transpose(w_uv, (0, 2, 1)).astype(BF16), bsz, s, ks)
    x1 = _mixout(proj, y_rnn, y_attn, x2d, w_o.astype(BF16), tm_mix)
    xn, s1t, s2t = _peerq(x1, row(norm_ffn_g), peer_wq.astype(BF16),
                          peer_keys1.astype(BF16), peer_keys2.astype(BF16), tm_pq)
    ids, gates = _route(s1t, s2t, tg)
    y = _experts(ids, gates, xn, x1, row(norm_final_g), _pack_uv(peer_u, peer_v), tg)
    return y.reshape(bsz, s, d)


def kernel(x, norm_mix_g, w_in, conv_w, conv_b, rg_wa, rg_ba, rg_wx, rg_bx, rg_lambda,
           kv_norm_g, w_uk, w_uv, idx_ln_g, idx_ln_b, w_o, norm_ffn_g, peer_wq,
           peer_keys1, peer_keys2, peer_u, peer_v, norm_final_g):
    assert norm_mix_g.shape[0] == 1, "single-layer trunk"
    s = x.shape[1]
    return _layer(
        x, norm_mix_g[0], w_in[0], conv_w[0], conv_b[0], rg_wa[0], rg_ba[0], rg_wx[0], rg_bx[0],
        rg_lambda[0], kv_norm_g[0], w_uk[0], w_uv[0], idx_ln_g[0], idx_ln_b[0], w_o[0],
        norm_ffn_g[0], peer_wq[0], peer_keys1[0], peer_keys2[0], peer_u[0], peer_v[0], norm_final_g,
        tm_in=min(1024, s), tn_in=1664, tc_rnn=min(512, s), tp=min(512, s), ks=min(512, s),
        tm_mix=min(256, s), tm_pq=min(256, s), tg=128)
```

```python
import functools

import jax
import jax.numpy as jnp
import numpy as np
from jax import lax
from jax.experimental import pallas as pl
from jax.experimental.pallas import tpu as pltpu

D_MODEL = 2048
RNN_BLOCKS = 16
RNN_BW = D_MODEL // RNN_BLOCKS
CONV_WIDTH = 4
RG_C = 8.0
N_HEADS = 16
HEAD_DIM = 128
KV_RANK = 512
KV_ROWS = KV_RANK + 16
IDX_HEADS = 16
IDX_DIM = 64
TOPK_MAX = 256
Q_BLOCK = 128
PEER_HEADS = 8
PEER_KEYS = 128
PEER_QDIM = 256
PEER_TOPK = 16
PEER_SLOTS = PEER_HEADS * PEER_TOPK
EPS = 1e-6

OFF_XR, OFF_Q, OFF_GR, OFF_GA = 0, 2048, 4096, 6144
OFF_QI, OFF_CKV, OFF_KI, OFF_WI = 8192, 9216, 9728, 9856
N_PROJ = 9984

V7X_VMEM_LIMIT = 56 * 1024 * 1024
COUNT_ROWS = 64
LOGIT_SCALE_LOG2 = HEAD_DIM ** -0.5 * float(np.log2(np.e))
INT_MIN = -(2 ** 31)
NEG_BIG = -1e30

BF16 = jnp.bfloat16
F32 = jnp.float32
NT_DIMS = (((1,), (1,)), ((), ()))


def _mm(a, b):
    return jnp.dot(a, b, preferred_element_type=F32)


def _mm_nt(a, b):
    return lax.dot_general(a, b, NT_DIMS, preferred_element_type=F32)


def _cparams(sem):
    return pltpu.CompilerParams(dimension_semantics=sem, vmem_limit_bytes=V7X_VMEM_LIMIT)


def _inproj_kernel(x_ref, g_ref, w_ref, o_ref, xn_ref):
    @pl.when(pl.program_id(1) == 0)
    def _():
        x = x_ref[...]
        ms = jnp.mean(x * x, axis=-1, keepdims=True)
        xn_ref[...] = ((x * lax.rsqrt(ms + EPS)) * g_ref[...]).astype(BF16)

    o_ref[...] = _mm(xn_ref[...], w_ref[...])


def _inproj(x2d, g, w_r, tm, tn):
    t, d = x2d.shape
    n = w_r.shape[1]
    return pl.pallas_call(
        _inproj_kernel,
        grid=(t // tm, n // tn),
        in_specs=[
            pl.BlockSpec((tm, d), lambda i, j: (i, 0)),
            pl.BlockSpec((1, d), lambda i, j: (0, 0)),
            pl.BlockSpec((d, tn), lambda i, j: (0, j)),
        ],
        out_specs=pl.BlockSpec((tm, tn), lambda i, j: (i, j)),
        out_shape=jax.ShapeDtypeStruct((t, n), F32),
        scratch_shapes=[pltpu.VMEM((tm, d), BF16)],
        compiler_params=_cparams(("parallel", "arbitrary")),
        name="inproj",
    )(x2d, g, w_r)


def _rglru_kernel(x_ref, cw_ref, cb_ref, wa_ref, ba_ref, wx_ref, bx_ref, lam_ref,
                  o_ref, xprev_ref, hprev_ref):
    @pl.when(pl.program_id(2) == 0)
    def _():
        xprev_ref[...] = jnp.zeros_like(xprev_ref)
        hprev_ref[...] = jnp.zeros_like(hprev_ref)

    x = x_ref[...]
    tc = x.shape[0]
    prev8 = xprev_ref[...]
    rows8 = lax.broadcasted_iota(jnp.int32, (8, RNN_BW), 0)
    y = cb_ref[...] + cw_ref[CONV_WIDTH - 1:CONV_WIDTH, :] * x
    for k in range(1, CONV_WIDTH):
        r = pltpu.roll(x, k, 0)
        pr = pltpu.roll(prev8, k, 0)
        top = jnp.where(rows8 < k, pr, r[:8])
        xs = jnp.concatenate([top, r[8:]], axis=0)
        y = y + cw_ref[CONV_WIDTH - 1 - k:CONV_WIDTH - k, :] * xs
    xprev_ref[...] = x[tc - 8:, :]

    xb = y.astype(BF16)
    r_g = jax.nn.sigmoid(_mm(xb, wa_ref[0]) + ba_ref[...])
    i_g = jax.nn.sigmoid(_mm(xb, wx_ref[0]) + bx_ref[...])
    nl = -lam_ref[...]
    sp = jnp.maximum(nl, 0.0) + jnp.log1p(jnp.exp(-jnp.abs(nl)))
    log_a = (-RG_C) * r_g * sp
    a = jnp.exp(log_a)
    u = jnp.sqrt(-jnp.tanh(log_a) * (a * a + 1.0)) * (i_g * y)

    rows = lax.broadcasted_iota(jnp.int32, (tc, RNN_BW), 0)
    d = 1
    while d < tc:
        a_sh = pltpu.roll(a, d, 0)
        u_sh = pltpu.roll(u, d, 0)
        m = rows >= d
        u = jnp.where(m, a * u_sh + u, u)
        a = jnp.where(m, a * a_sh, a)
        d *= 2
    h = u + a * hprev_ref[...]
    o_ref[...] = h
    hprev_ref[...] = h[tc - 1:tc, :]


def _rglru(proj, conv_w, conv_b, wa, ba, wx, bx, lam, bsz, s, tc):
    t = bsz * s
    nc = s // tc
    vec = lambda: pl.BlockSpec((1, RNN_BW), lambda b, n, c: (0, n))
    return pl.pallas_call(
        _rglru_kernel,
        grid=(bsz, RNN_BLOCKS, nc),
        in_specs=[
            pl.BlockSpec((tc, RNN_BW), lambda b, n, c: (b * nc + c, n)),
            pl.BlockSpec((CONV_WIDTH, RNN_BW), lambda b, n, c: (0, n)),
            vec(),
            pl.BlockSpec((1, RNN_BW, RNN_BW), lambda b, n, c: (n, 0, 0)),
            vec(),
            pl.BlockSpec((1, RNN_BW, RNN_BW), lambda b, n, c: (n, 0, 0)),
            vec(),
            vec(),
        ],
        out_specs=pl.BlockSpec((tc, RNN_BW), lambda b, n, c: (b * nc + c, n)),
        out_shape=jax.ShapeDtypeStruct((t, D_MODEL), F32),
        scratch_shapes=[pltpu.VMEM((8, RNN_BW), F32), pltpu.VMEM((1, RNN_BW), F32)],
        compiler_params=_cparams(("parallel", "parallel", "arbitrary")),
        name="rglru",
    )(proj, conv_w, conv_b, wa, ba, wx, bx, lam)


def _prep_kernel(ckv_ref, ki_ref, wi_ref, kvg_ref, lng_ref, lnb_ref,
                 ckvn_ref, ckvt_ref, kin_ref, wit_ref):
    c = ckv_ref[...]
    cn = (c * lax.rsqrt(jnp.mean(c * c, axis=-1, keepdims=True) + EPS)) * kvg_ref[...]
    ckvn_ref[...] = cn.astype(BF16)
    ckvt_ref[0, :KV_RANK, :] = cn.T.astype(BF16)
    ones_row = lax.broadcasted_iota(jnp.int32, (KV_ROWS - KV_RANK, c.shape[0]), 0) == 0
    ckvt_ref[0, KV_RANK:, :] = jnp.where(ones_row, 1.0, 0.0).astype(BF16)
    k = ki_ref[...][:, :IDX_DIM]
    mu = jnp.mean(k, axis=-1, keepdims=True)
    var = jnp.mean(jnp.square(k - mu), axis=-1, keepdims=True)
    kn = (k - mu) * lax.rsqrt(var + EPS)
    kin_ref[...] = (kn * lng_ref[...] + lnb_ref[...]).astype(BF16)
    w = wi_ref[...] * (IDX_HEADS ** -0.5 * IDX_DIM ** -0.5)
    wit_ref[0] = w.T[:IDX_HEADS, :]


def _prep(proj, kvg, lng, lnb, bsz, s, tp):
    t = bsz * s
    nc = s // tp
    return pl.pallas_call(
        _prep_kernel,
        grid=(bsz, nc),
        in_specs=[
            pl.BlockSpec((tp, KV_RANK), lambda b, c: (b * nc + c, OFF_CKV // KV_RANK)),
            pl.BlockSpec((tp, 128), lambda b, c: (b * nc + c, OFF_KI // 128)),
            pl.BlockSpec((tp, 128), lambda b, c: (b * nc + c, OFF_WI // 128)),
            pl.BlockSpec((1, KV_RANK), lambda b, c: (0, 0)),
            pl.BlockSpec((1, IDX_DIM), lambda b, c: (0, 0)),
            pl.BlockSpec((1, IDX_DIM), lambda b, c: (0, 0)),
        ],
        out_specs=[
            pl.BlockSpec((tp, KV_RANK), lambda b, c: (b * nc + c, 0)),
            pl.BlockSpec((1, KV_ROWS, tp), lambda b, c: (b, 0, c)),
            pl.BlockSpec((tp, IDX_DIM), lambda b, c: (b * nc + c, 0)),
            pl.BlockSpec((1, IDX_HEADS, tp), lambda b, c: (b, 0, c)),
        ],
        out_shape=[
            jax.ShapeDtypeStruct((t, KV_RANK), BF16),
            jax.ShapeDtypeStruct((bsz, KV_ROWS, s), BF16),
            jax.ShapeDtypeStruct((t, IDX_DIM), BF16),
            jax.ShapeDtypeStruct((bsz, IDX_HEADS, s), F32),
        ],
        compiler_params=_cparams(("parallel", "parallel")),
        name="prep",
    )(proj, proj, proj, kvg, lng, lnb)


def _dsa_kernel(q_ref, qi_ref, wit_ref, k_ref, ckv_ref, ckvt_ref, wuk_ref, wuvt_ref, o_ref,
                keys_sc, qit_sc, qlat_sc, acc_sc, m_sc, alpha_sc, p_sc, *, ks, topk):
    qb = pl.program_id(1)
    t0 = qb * Q_BLOCK
    nkb = (t0 + Q_BLOCK + ks - 1) // ks
    hd = HEAD_DIM

    for h in range(N_HEADS):
        qh = q_ref[:, h * hd:(h + 1) * hd].astype(BF16)
        qlat_sc[:, h * Q_BLOCK:(h + 1) * Q_BLOCK] = (_mm_nt(wuk_ref[h], qh) * LOGIT_SCALE_LOG2).astype(BF16)
    eye = jnp.where(lax.broadcasted_iota(jnp.int32, (IDX_DIM, IDX_DIM), 0)
                    == lax.broadcasted_iota(jnp.int32, (IDX_DIM, IDX_DIM), 1), 1.0, 0.0).astype(BF16)
    for h in range(IDX_HEADS):
        qih = qi_ref[:, h * IDX_DIM:(h + 1) * IDX_DIM].astype(BF16)
        qit_sc[:, h * Q_BLOCK:(h + 1) * Q_BLOCK] = _mm_nt(eye, qih).astype(BF16)

    lane_q = t0 + lax.broadcasted_iota(jnp.int32, (ks, Q_BLOCK), 1)
    row_i = lax.broadcasted_iota(jnp.int32, (ks, Q_BLOCK), 0)

    def score_body(kb, carry):
        s0 = pl.multiple_of(kb * ks, ks)
        kblk = k_ref[0, pl.ds(s0, ks), :]
        r_all = _mm(kblk, qit_sc[...])
        acc = jnp.zeros((ks, Q_BLOCK), F32)
        for h in range(IDX_HEADS):
            r = r_all[:, h * Q_BLOCK:(h + 1) * Q_BLOCK]
            acc = acc + jnp.maximum(r, 0.0) * wit_ref[0, h:h + 1, :]
        bits = pltpu.bitcast(acc, jnp.int32)
        key = bits ^ ((bits >> 31) & jnp.int32(0x7FFFFFFF))
        key = jnp.where(s0 + row_i <= lane_q, key, jnp.int32(INT_MIN))
        keys_sc[pl.ds(s0, ks), :] = key
        return carry

    lax.fori_loop(0, nkb, score_body, 0)

    def count_ge(cand):
        def body(kb, c):
            s0 = pl.multiple_of(kb * ks, ks)
            blk = keys_sc[pl.ds(s0, ks), :]
            hit = jnp.where(blk >= cand, 1.0, 0.0)
            return c + jnp.sum(hit.reshape(ks // COUNT_ROWS, COUNT_ROWS, Q_BLOCK), axis=0)
        cpart = lax.fori_loop(0, nkb, body, jnp.zeros((COUNT_ROWS, Q_BLOCK), F32))
        return jnp.sum(cpart, axis=0, keepdims=True)

    def bit_body(i, tu):
        bit = lax.shift_left(jnp.int32(1), 31 - i)
        cand_u = tu | bit
        cnt = count_ge(cand_u ^ jnp.int32(INT_MIN))
        return jnp.where(cnt >= float(topk), cand_u, tu)

    tu = lax.fori_loop(0, 32, bit_body, jnp.zeros((1, Q_BLOCK), jnp.int32))
    tsel = jnp.maximum(tu ^ jnp.int32(INT_MIN), jnp.int32(INT_MIN + 1))

    m_sc[...] = jnp.full_like(m_sc, NEG_BIG)
    acc_sc[...] = jnp.zeros_like(acc_sc)

    def attn_body(kb, carry):
        s0 = pl.multiple_of(kb * ks, ks)
        cblk = ckv_ref[0, pl.ds(s0, ks), :]
        logt = _mm(cblk, qlat_sc[...])
        sel = keys_sc[pl.ds(s0, ks), :] >= tsel
        for h in range(N_HEADS):
            sl = slice(h * Q_BLOCK, (h + 1) * Q_BLOCK)
            lg = jnp.where(sel, logt[:, sl], NEG_BIG)
            mb = jnp.max(lg, axis=0, keepdims=True)
            mo = m_sc[:, sl]
            mn = jnp.maximum(mo, mb)
            p = jnp.exp2(lg - mn)
            alpha = jnp.exp2(mo - mn)
            m_sc[:, sl] = mn
            alpha_sc[:, sl] = alpha
            p_sc[:, sl] = p.astype(BF16)
        ctb = ckvt_ref[0, :, pl.ds(s0, ks)]
        acc_sc[...] = acc_sc[...] * alpha_sc[...] + _mm(ctb, p_sc[...])
        return carry

    lax.fori_loop(0, nkb, attn_body, 0)

    inv_l = 1.0 / acc_sc[KV_RANK:KV_RANK + 1, :]
    for h in range(N_HEADS):
        sl = slice(h * Q_BLOCK, (h + 1) * Q_BLOCK)
        ol = (acc_sc[:KV_RANK, sl] * inv_l[:, sl]).astype(BF16)
        ot = _mm(wuvt_ref[h], ol)
        o_ref[:, h * hd:(h + 1) * hd] = ot.T


def _dsa(proj, wit, kin, ckvn, ckvt, wuk, wuvt, bsz, s, ks):
    t = bsz * s
    nq = s // Q_BLOCK
    topk = min(TOPK_MAX, s // 4)
    one = pl.Buffered(1)
    kern = functools.partial(_dsa_kernel, ks=ks, topk=topk)
    return pl.pallas_call(
        kern,
        grid=(bsz, nq),
        in_specs=[
            pl.BlockSpec((Q_BLOCK, D_MODEL), lambda b, i: (b * nq + i, OFF_Q // D_MODEL)),
            pl.BlockSpec((Q_BLOCK, IDX_HEADS * IDX_DIM), lambda b, i: (b * nq + i, OFF_QI // 1024)),
            pl.BlockSpec((1, IDX_HEADS, Q_BLOCK), lambda b, i: (b, 0, i)),
            pl.BlockSpec((1, s, IDX_DIM), lambda b, i: (b, 0, 0), pipeline_mode=one),
            pl.BlockSpec((1, s, KV_RANK), lambda b, i: (b, 0, 0), pipeline_mode=one),
            pl.BlockSpec((1, KV_ROWS, s), lambda b, i: (b, 0, 0), pipeline_mode=one),
            pl.BlockSpec((N_HEADS, KV_RANK, HEAD_DIM), lambda b, i: (0, 0, 0), pipeline_mode=one),
            pl.BlockSpec((N_HEADS, HEAD_DIM, KV_RANK), lambda b, i: (0, 0, 0), pipeline_mode=one),
        ],
        out_specs=pl.BlockSpec((Q_BLOCK, D_MODEL), lambda b, i: (b * nq + i, 0)),
        out_shape=jax.ShapeDtypeStruct((t, D_MODEL), F32),
        scratch_shapes=[
            pltpu.VMEM((s, Q_BLOCK), jnp.int32),
            pltpu.VMEM((IDX_DIM, IDX_HEADS * Q_BLOCK), BF16),
            pltpu.VMEM((KV_RANK, N_HEADS * Q_BLOCK), BF16),
            pltpu.VMEM((KV_ROWS, N_HEADS * Q_BLOCK), F32),
            pltpu.VMEM((1, N_HEADS * Q_BLOCK), F32),
            pltpu.VMEM((1, N_HEADS * Q_BLOCK), F32),
            pltpu.VMEM((ks, N_HEADS * Q_BLOCK), BF16),
        ],
        compiler_params=_cparams(("parallel", "arbitrary")),
        name="dsa",
    )(proj, proj, wit, kin.reshape(bsz, s, IDX_DIM), ckvn.reshape(bsz, s, KV_RANK), ckvt, wuk, wuvt)


def _mixout_kernel(gr_ref, ga_ref, yr_ref, ya_ref, x_ref, wo_ref, x1_ref):
    mixed = jax.nn.sigmoid(gr_ref[...]) * yr_ref[...] + jax.nn.sigmoid(ga_ref[...]) * ya_ref[...]
    x1_ref[...] = x_ref[...] + _mm(mixed.astype(BF16), wo_ref[...])


def _mixout(proj, y_rnn, y_attn, x2d, wo, tm):
    t, d = x2d.shape
    row = lambda i: (i, 0)
    return pl.pallas_call(
        _mixout_kernel,
        grid=(t // tm,),
        in_specs=[
            pl.BlockSpec((tm, d), lambda i: (i, OFF_GR // D_MODEL)),
            pl.BlockSpec((tm, d), lambda i: (i, OFF_GA // D_MODEL)),
            pl.BlockSpec((tm, d), row),
            pl.BlockSpec((tm, d), row),
            pl.BlockSpec((tm, d), row),
            pl.BlockSpec((d, d), lambda i: (0, 0), pipeline_mode=pl.Buffered(1)),
        ],
        out_specs=pl.BlockSpec((tm, d), row),
        out_shape=jax.ShapeDtypeStruct((t, d), F32),
        compiler_params=_cparams(("parallel",)),
        name="mixout",
    )(proj, proj, y_rnn, y_attn, x2d, wo)


def _peerq_kernel(x1_ref, g_ref, wq_ref, k1_ref, k2_ref, xn_ref, s1_ref, s2_ref):
    x = x1_ref[...]
    xn = (x * lax.rsqrt(jnp.mean(x * x, axis=-1, keepdims=True) + EPS)) * g_ref[...]
    xn_ref[...] = xn
    qp = _mm(xn.astype(BF16), wq_ref[...])
    half = PEER_QDIM // 2
    for h in range(PEER_HEADS):
        qa = qp[:, h * PEER_QDIM:h * PEER_QDIM + half].astype(BF16)
        qb = qp[:, h * PEER_QDIM + half:(h + 1) * PEER_QDIM].astype(BF16)
        s1_ref[h] = _mm_nt(k1_ref[...], qa)
        s2_ref[h] = _mm_nt(k2_ref[...], qb)


def _peerq(x1, g, wq, k1, k2, tm):
    t, d = x1.shape
    sc_spec = pl.BlockSpec((PEER_HEADS, PEER_KEYS, tm), lambda i: (0, 0, i))
    sc_shape = jax.ShapeDtypeStruct((PEER_HEADS, PEER_KEYS, t), F32)
    return pl.pallas_call(
        _peerq_kernel,
        grid=(t // tm,),
        in_specs=[
            pl.BlockSpec((tm, d), lambda i: (i, 0)),
            pl.BlockSpec((1, d), lambda i: (0, 0)),
            pl.BlockSpec((d, PEER_HEADS * PEER_QDIM), lambda i: (0, 0), pipeline_mode=pl.Buffered(1)),
            pl.BlockSpec((PEER_KEYS, PEER_QDIM // 2), lambda i: (0, 0)),
            pl.BlockSpec((PEER_KEYS, PEER_QDIM // 2), lambda i: (0, 0)),
        ],
        out_specs=[pl.BlockSpec((tm, d), lambda i: (i, 0)), sc_spec, sc_spec],
        out_shape=[jax.ShapeDtypeStruct((t, d), F32), sc_shape, sc_shape],
        compiler_params=_cparams(("parallel",)),
        name="peerq",
    )(x1, g, wq, k1, k2)


def _top16_rows(v, pos, payload=None):
    vals, poss, pays = [], [], []
    for _ in range(PEER_TOPK):
        m = jnp.max(v, axis=0, keepdims=True)
        p = jnp.min(jnp.where(v == m, pos, jnp.inf), axis=0, keepdims=True)
        hit = pos == p
        vals.append(m)
        poss.append(p)
        if payload is not None:
            pays.append(jnp.sum(jnp.where(hit, payload, 0), axis=0, keepdims=True))
        v = jnp.where(hit, -jnp.inf, v)
    return vals, poss, pays


def _route_kernel(s1_ref, s2_ref, ids_ref, g_ref):
    tg = s1_ref.shape[2]
    k = PEER_TOPK
    key_pos = lax.broadcasted_iota(jnp.int32, (PEER_KEYS, tg), 0).astype(F32)
    b8 = lax.broadcasted_iota(jnp.int32, (8, tg), 0).astype(F32)
    b16 = lax.broadcasted_iota(jnp.int32, (k, tg), 0).astype(F32)
    cand_pos = jnp.concatenate(
        [b16] + [b8 + float(a * k) for a in range(1, 8)] + [(b8 + 8.0) * float(k)], axis=0)
    for h in range(PEER_HEADS):
        v1, p1, _ = _top16_rows(s1_ref[h], key_pos)
        v2, p2, _ = _top16_rows(s2_ref[h], key_pos)
        v1c = jnp.concatenate(v1, axis=0)
        v2c = jnp.concatenate(v2, axis=0)
        i1c = jnp.concatenate(p1, axis=0).astype(jnp.int32) * PEER_KEYS
        i2c = jnp.concatenate(p2, axis=0).astype(jnp.int32)
        cand_s = jnp.concatenate(
            [v1c[0:1] + v2c] + [v1c[a:a + 1] + v2c[:8] for a in range(1, 8)] + [v1c[8:] + v2c[0:1]],
            axis=0)
        cand_i = jnp.concatenate(
            [i1c[0:1] + i2c] + [i1c[a:a + 1] + i2c[:8] for a in range(1, 8)] + [i1c[8:] + i2c[0:1]],
            axis=0)
        top_s, _, experts = _top16_rows(cand_s, cand_pos, payload=cand_i)
        ts = jnp.concatenate(top_s, axis=0)
        e = jnp.exp(ts - ts[0:1, :])
        gate = e / jnp.sum(e, axis=0, keepdims=True)
        ids_ref[0, h * k:(h + 1) * k, :] = jnp.concatenate(experts, axis=0)
        g_ref[0, h * k:(h + 1) * k, :] = gate


def _route(s1t, s2t, tg):
    t = s1t.shape[2]
    ng = t // tg
    sc_spec = pl.BlockSpec((PEER_HEADS, PEER_KEYS, tg), lambda i: (0, 0, i))
    out_spec = pl.BlockSpec((1, PEER_SLOTS, tg), lambda i: (i, 0, 0))
    return pl.pallas_call(
        _route_kernel,
        grid=(ng,),
        in_specs=[sc_spec, sc_spec],
        out_specs=[out_spec, out_spec],
        out_shape=[jax.ShapeDtypeStruct((ng, PEER_SLOTS, tg), jnp.int32),
                   jax.ShapeDtypeStruct((ng, PEER_SLOTS, tg), F32)],
        compiler_params=_cparams(("parallel",)),
        name="route",
    )(s1t, s2t)


N_GATHER_BUFS = 8
GATHER_AHEAD = N_GATHER_BUFS - 1
U_PHASE_ISSUE = 1
ISSUE_EVERY = 2
LANE = 128
N_CHUNKS = D_MODEL // LANE


class _RowIssuer:
    def __init__(self, start_row):
        self._start_row, self._next = start_row, 0

    def issue(self, n):
        for j in range(self._next, min(self._next + n, PEER_SLOTS)):
            self._start_row(j)
        self._next = min(self._next + n, PEER_SLOTS)


def _experts_kernel(ids_ref, idsn_ref, g_ref, xn_ref, x1_ref, gf_ref, uv_ref, y_ref, buf, sem, acc_ref, *, tg):
    i = pl.program_id(0)
    nb = N_GATHER_BUFS
    lane_t = lax.broadcasted_iota(jnp.int32, (PEER_SLOTS, tg), 1)

    def start_row(src_ids, tok, j, slot):
        e = src_ids[0, tok, j]
        prio = j % 2 if isinstance(j, int) else 0
        pltpu.make_async_copy(uv_ref.at[e], buf.at[slot, pl.ds(j, 1), :], sem.at[slot]).start(priority=prio)

    def wait_token(slot):
        pltpu.make_async_copy(buf.at[slot], buf.at[slot], sem.at[slot]).wait()

    @pl.when(i == 0)
    def _():
        for p in range(GATHER_AHEAD):
            def row(j, carry, p=p):
                start_row(ids_ref, p, j, p)
                return carry
            lax.fori_loop(0, PEER_SLOTS, row, 0)

    def u_phase(tok, slot, issuer, per_chunk):
        xrow = xn_ref[pl.ds(tok, 1), :]
        zpart = jnp.zeros((PEER_SLOTS, LANE), F32)
        for c in range(N_CHUNKS):
            sl = slice(c * LANE, (c + 1) * LANE)
            w = buf[slot, :, sl]
            zpart = zpart + pltpu.bitcast(w & jnp.uint32(0xFFFF0000), F32) * xrow[:, sl]
            if (c + 1) % ISSUE_EVERY == 0:
                issuer.issue(per_chunk * ISSUE_EVERY)
        z = jnp.sum(zpart, axis=-1, keepdims=True)
        gcol = jnp.sum(jnp.where(lane_t == tok, g_ref[0], 0.0), axis=-1, keepdims=True)
        return z, gcol

    def v_phase(tok, slot, act, issuer, per_chunk):
        out = []
        for c in range(N_CHUNKS):
            w = buf[slot, :, c * LANE:(c + 1) * LANE]
            out.append(jnp.sum(act * pltpu.bitcast(w << 16, F32), axis=0, keepdims=True))
            if (c + 1) % ISSUE_EVERY == 0:
                issuer.issue(per_chunk * ISSUE_EVERY)
        acc_ref[pl.ds(tok, 1), :] = jnp.concatenate(out, axis=-1)

    def activation(z, gcol):
        return (0.5 * z * (1.0 + lax.erf(z * (2.0 ** -0.5)))) * gcol

    def step(tok, r, act, src_ids, ntok, has_next):
        issuer = _RowIssuer(lambda j: start_row(src_ids, ntok, j, (r + GATHER_AHEAD) % nb))
        if has_next:
            wait_token((r + 1) % nb)
            z, gcol = u_phase(tok + 1, (r + 1) % nb, issuer, U_PHASE_ISSUE)
            v_phase(tok, r, act, issuer, PEER_SLOTS // N_CHUNKS - U_PHASE_ISSUE)
            return activation(z, gcol)
        v_phase(tok, r, act, issuer, PEER_SLOTS // N_CHUNKS)
        return act

    wait_token(0)
    act0 = activation(*u_phase(0, 0, _RowIssuer(lambda j: None), 0))

    def main_body(q, act):
        for r in range(nb):
            tok = q * nb + r
            act = step(tok, r, act, ids_ref, tok + GATHER_AHEAD, True)
        return act

    act = lax.fori_loop(0, tg // nb - 1, main_body, act0)
    for tok in range(tg - nb, tg):
        ntok = tok + GATHER_AHEAD
        src_ids, ntok = (ids_ref, ntok) if ntok < tg else (idsn_ref, ntok - tg)
        act = step(tok, tok % nb, act, src_ids, ntok, tok + 1 < tg)

    @pl.when(i == pl.num_programs(0) - 1)
    def _():
        for p in range(GATHER_AHEAD):
            wait_token(p)

    xf = x1_ref[...] + acc_ref[...]
    y_ref[...] = (xf * lax.rsqrt(jnp.mean(xf * xf, axis=-1, keepdims=True) + EPS)) * gf_ref[...]


def _experts(ids, gates, xn, x1, gf, uv, tg):
    t, d = x1.shape
    ng = t // tg
    assert tg % N_GATHER_BUFS == 0 and tg > GATHER_AHEAD
    ids_tok = jnp.transpose(ids, (0, 2, 1))
    kern = functools.partial(_experts_kernel, tg=tg)
    return pl.pallas_call(
        kern,
        grid=(ng,),
        in_specs=[
            pl.BlockSpec((1, tg, PEER_SLOTS), lambda i: (i, 0, 0), memory_space=pltpu.SMEM),
            pl.BlockSpec((1, tg, PEER_SLOTS), lambda i: (jnp.minimum(i + 1, ng - 1), 0, 0),
                         memory_space=pltpu.SMEM),
            pl.BlockSpec((1, PEER_SLOTS, tg), lambda i: (i, 0, 0)),
            pl.BlockSpec((tg, d), lambda i: (i, 0)),
            pl.BlockSpec((tg, d), lambda i: (i, 0)),
            pl.BlockSpec((1, d), lambda i: (0, 0)),
            pl.BlockSpec(memory_space=pl.ANY),
        ],
        out_specs=pl.BlockSpec((tg, d), lambda i: (i, 0)),
        out_shape=jax.ShapeDtypeStruct((t, d), F32),
        scratch_shapes=[
            pltpu.VMEM((N_GATHER_BUFS, PEER_SLOTS, d), jnp.uint32),
            pltpu.SemaphoreType.DMA((N_GATHER_BUFS,)),
            pltpu.VMEM((tg, d), F32),
        ],
        compiler_params=_cparams(("arbitrary",)),
        name="experts",
    )(ids_tok, ids_tok, gates, xn, x1, gf, uv)


def _regroup_w_in(w):
    d = w.shape[0]
    z = lambda n: jnp.zeros((d, n), w.dtype)
    xr_q = w[:, 0:4096]
    ckv = w[:, 4096:4608]
    qi = w[:, 4608:5632]
    ki = w[:, 5632:5696]
    wi = w[:, 5696:5712]
    gr_ga = w[:, 5712:9808]
    return jnp.concatenate([xr_q, gr_ga, qi, ckv, ki, z(64), wi, z(112)], axis=1).astype(BF16)


def _pack_uv(u, v):
    ub = lax.bitcast_convert_type(u.astype(BF16), jnp.uint16).astype(jnp.uint32)
    vb = lax.bitcast_convert_type(v.astype(BF16), jnp.uint16).astype(jnp.uint32)
    return ((ub << 16) | vb)[:, None, :]


def _layer(x, norm_mix_g, w_in, conv_w, conv_b, rg_wa, rg_ba, rg_wx, rg_bx, rg_lambda,
           kv_norm_g, w_uk, w_uv, idx_ln_g, idx_ln_b, w_o, norm_ffn_g, peer_wq,
           peer_keys1, peer_keys2, peer_u, peer_v, norm_final_g, *,
           tm_in, tn_in, tc_rnn, tp, ks, tm_mix, tm_pq, tg):
    bsz, s, d = x.shape
    t = bsz * s
    x2d = x.reshape(t, d)
    row = lambda a: a.reshape(1, -1)

    proj = _inproj(x2d, row(norm_mix_g), _regroup_w_in(w_in), tm_in, tn_in)
    y_rnn = _rglru(proj, conv_w, row(conv_b), rg_wa.astype(BF16), row(rg_ba),
                   rg_wx.astype(BF16), row(rg_bx), row(rg_lambda), bsz, s, tc_rnn)
    ckvn, ckvt, kin, wit = _prep(proj, row(kv_norm_g), row(idx_ln_g), row(idx_ln_b), bsz, s, tp)
    y_attn = _dsa(proj, wit, kin, ckvn, ckvt, w_uk.astype(BF16),
                  jnp.transpose(w_uv, (0, 2, 1)).astype(BF16), bsz, s, ks)
    x1 = _mixout(proj, y_rnn, y_attn, x2d, w_o.astype(BF16), tm_mix)
    xn, s1t, s2t = _peerq(x1, row(norm_ffn_g), peer_wq.astype(BF16),
                          peer_keys1.astype(BF16), peer_keys2.astype(BF16), tm_pq)
    ids, gates = _route(s1t, s2t, tg)
    y = _experts(ids, gates, xn, x1, row(norm_final_g), _pack_uv(peer_u, peer_v), tg)
    return y.reshape(bsz, s, d)


def kernel(x, norm_mix_g, w_in, conv_w, conv_b, rg_wa, rg_ba, rg_wx, rg_bx, rg_lambda,
           kv_norm_g, w_uk, w_uv, idx_ln_g, idx_ln_b, w_o, norm_ffn_g, peer_wq,
           peer_keys1, peer_keys2, peer_u, peer_v, norm_final_g):
    assert norm_mix_g.shape[0] == 1, "single-layer trunk"
    s = x.shape[1]
    return _layer(
        x, norm_mix_g[0], w_in[0], conv_w[0], conv_b[0], rg_wa[0], rg_ba[0], rg_wx[0], rg_bx[0],
        rg_lambda[0], kv_norm_g[0], w_uk[0], w_uv[0], idx_ln_g[0], idx_ln_b[0], w_o[0],
        norm_ffn_g[0], peer_wq[0], peer_keys1[0], peer_keys2[0], peer_u[0], peer_v[0], norm_final_g,
        tm_in=min(1024, s), tn_in=1664, tc_rnn=min(512, s), tp=min(512, s), ks=min(512, s),
        tm_mix=min(256, s), tm_pq=min(256, s), tg=128)
```
